```python
import jax, jax.numpy as jnp
from jax import lax
import numpy as np

D_MODEL = 1024
BATCH = 32
SEQ = 2048
DEPTH = 4

HEAD_DIM = 64
MLA_HEADS = 8
MLA_Q_RANK = 256
MLA_KV_RANK = 128
MLA_NOPE = 64
MLA_ROPE = 32
MLA_V = 64
DSA_HEADS = 4
IDX_HEADS = 8
IDX_DIM = 32
DSA_TOPK_MAX = 256
DIL_CONFIGS = ((128, 1), (512, 4), (2048, 16))
DIL_HEADS = 4
MIX_WIDTH = (MLA_HEADS + DSA_HEADS + DIL_HEADS) * HEAD_DIM
ROPE_THETA = 500000.0
ROT_HEAD_DIMS = HEAD_DIM // 4
ROT_IDX_DIMS = IDX_DIM // 4
Q_BLOCK = 128
N_GROUPS = 4
EXPERTS_PER_GROUP = 8
TOP_K_IN_GROUP = 2
EXPERT_HIDDEN = 256
DEEPNORM_ALPHA = (2.0 * DEPTH) ** 0.25
DEEPNORM_BETA = (8.0 * DEPTH) ** -0.25
LN_EPS = 1e-5
RMS_EPS = 1e-6
NEG_INF = -1e30
IN_SPLITS = (MLA_Q_RANK, MLA_KV_RANK, MLA_ROPE,
             DSA_HEADS * HEAD_DIM, HEAD_DIM, HEAD_DIM,
             IDX_HEADS * IDX_DIM, IDX_DIM, IDX_HEADS,
             3 * len(DIL_CONFIGS) * DIL_HEADS * HEAD_DIM)
IN_COLS = sum(IN_SPLITS)

kernel_name = "hybrid_mla_dsa_dilated_hmoe_deepnorm"


def layer_norm(x, g, b):
    xf = x.astype(jnp.float32)
    mu = jnp.mean(xf, -1, keepdims=True)
    var = jnp.mean(jnp.square(xf - mu), -1, keepdims=True)
    return ((xf - mu) * lax.rsqrt(var + LN_EPS) * g.astype(jnp.float32) + b.astype(jnp.float32)).astype(x.dtype)


def rms_norm(x, g):
    xf = x.astype(jnp.float32)
    y = xf * lax.rsqrt(jnp.mean(jnp.square(xf), -1, keepdims=True) + RMS_EPS)
    return (y * g.astype(jnp.float32)).astype(x.dtype)


def rope_table(positions, rot_dims):
    inv = ROPE_THETA ** (-jnp.arange(0, rot_dims, 2, dtype=jnp.float32) / rot_dims)
    ang = positions.astype(jnp.float32)[..., None] * inv
    return jnp.cos(ang), jnp.sin(ang)


def apply_rope(x, cos, sin):
    half = cos.shape[-1]
    r = 2 * half
    c, s = cos[:, :, None, :], sin[:, :, None, :]
    xf = x[..., :r].astype(jnp.float32)
    x1, x2 = xf[..., :half], xf[..., half:]
    rot = jnp.concatenate([x1 * c - x2 * s, x2 * c + x1 * s], -1).astype(x.dtype)
    return jnp.concatenate([rot, x[..., r:]], -1)


def _to_blocks(a):
    B, S = a.shape[:2]
    return jnp.moveaxis(a.reshape(B, S // Q_BLOCK, Q_BLOCK, *a.shape[2:]), 1, 0)


def _from_blocks(a):
    nb, B, qb = a.shape[:3]
    return jnp.moveaxis(a, 0, 1).reshape(B, nb * qb, *a.shape[3:])


def _block_starts(S):
    return jnp.arange(S // Q_BLOCK, dtype=jnp.int32) * Q_BLOCK


def mla_mixer(cq, ckv, kpe, q_norm, kv_norm, w_uq, w_ukv, cos, sin):
    B, S = cq.shape[:2]
    q = jnp.einsum('bsr,rc->bsc', rms_norm(cq, q_norm), w_uq).reshape(B, S, MLA_HEADS, MLA_NOPE + MLA_ROPE)
    q = jnp.concatenate([q[..., :MLA_NOPE], apply_rope(q[..., MLA_NOPE:], cos, sin)], -1)
    kv = jnp.einsum('bsr,rc->bsc', rms_norm(ckv, kv_norm), w_ukv).reshape(B, S, MLA_HEADS, MLA_NOPE + MLA_V)
    k_pe = apply_rope(kpe[:, :, None, :], cos, sin)
    k = jnp.concatenate([kv[..., :MLA_NOPE], jnp.broadcast_to(k_pe, (B, S, MLA_HEADS, MLA_ROPE))], -1)
    v = kv[..., MLA_NOPE:]
    scale = (MLA_NOPE + MLA_ROPE) ** -0.5
    key_pos = jnp.arange(S, dtype=jnp.int32)

    def block(args):
        q0, qb = args
        qpos = q0 + jnp.arange(Q_BLOCK, dtype=jnp.int32)
        s = jnp.einsum('bqhd,bkhd->bhqk', qb, k, preferred_element_type=jnp.float32) * scale
        s = jnp.where(key_pos[None, :] <= qpos[:, None], s, NEG_INF)
        p = jax.nn.softmax(s, -1).astype(v.dtype)
        return jnp.einsum('bhqk,bkhd->bqhd', p, v)

    o = _from_blocks(lax.map(block, (_block_starts(S), _to_blocks(q))))
    return o.reshape(B, S, MLA_HEADS * MLA_V)


def dsa_mixer(q, k, v, qi, ki, wi, cos_h, sin_h, cos_i, sin_i):
    B, S = q.shape[:2]
    q = apply_rope(q.reshape(B, S, DSA_HEADS, HEAD_DIM), cos_h, sin_h)
    k = apply_rope(k[:, :, None, :], cos_h, sin_h)[:, :, 0]
    qi = apply_rope(qi.reshape(B, S, IDX_HEADS, IDX_DIM), cos_i, sin_i)
    ki = apply_rope(ki[:, :, None, :], cos_i, sin_i)[:, :, 0]
    top_k = min(DSA_TOPK_MAX, S // 4)
    key_pos = jnp.arange(S, dtype=jnp.int32)
    gather = jax.vmap(lambda arr, idx: arr[idx])
    idx_scale = IDX_DIM ** -0.5 * IDX_HEADS ** -0.5

    def block(args):
        q0, qb, qib, wb = args
        qpos = q0 + jnp.arange(Q_BLOCK, dtype=jnp.int32)
        rel = jax.nn.relu(jnp.einsum('bqhd,bsd->bqhs', qib, ki, preferred_element_type=jnp.float32))
        score = jnp.einsum('bqhs,bqh->bqs', rel, wb.astype(jnp.float32)) * idx_scale
        score = jnp.where(key_pos[None, None, :] <= qpos[None, :, None], score, NEG_INF)
        _, sel = lax.top_k(score, top_k)
        k_sel = gather(k, sel)
        v_sel = gather(v, sel)
        valid = sel <= qpos[None, :, None]
        s = jnp.einsum('bqhd,bqkd->bhqk', qb, k_sel, preferred_element_type=jnp.float32) * HEAD_DIM ** -0.5
        s = jnp.where(valid[:, None], s, NEG_INF)
        p = jax.nn.softmax(s, -1).astype(v.dtype)
        return jnp.einsum('bhqk,bqkd->bqhd', p, v_sel)

    o = _from_blocks(lax.map(block, (_block_starts(S), _to_blocks(q), _to_blocks(qi), _to_blocks(wi))))
    return o.reshape(B, S, DSA_HEADS * HEAD_DIM)


def dilated_window_attention(q, k, v, window, dilation):
    S = q.shape[1]
    n_keys = window // dilation + 1
    offs = dilation * jnp.arange(n_keys, dtype=jnp.int32)

    def block(args):
        q0, qb = args
        qpos = q0 + jnp.arange(Q_BLOCK, dtype=jnp.int32)
        idx = qpos[:, None] - offs[None, :]
        valid = idx >= 0
        idx = jnp.maximum(idx, 0)
        k_sel = k[:, idx]
        v_sel = v[:, idx]
        s = jnp.einsum('bqhd,bqkhd->bqhk', qb, k_sel, preferred_element_type=jnp.float32) * HEAD_DIM ** -0.5
        s = jnp.where(valid[None, :, None, :], s, NEG_INF)
        lse = jax.nn.logsumexp(s, -1)
        p = jnp.exp(s - lse[..., None]).astype(v.dtype)
        return jnp.einsum('bqhk,bqkhd->bqhd', p, v_sel), lse

    o, lse = lax.map(block, (_block_starts(S), _to_blocks(q)))
    return _from_blocks(o), _from_blocks(lse)


def dilated_mixer(qkv, cos, sin):
    B, S = qkv.shape[:2]
    n_cfg = len(DIL_CONFIGS)
    qkv = qkv.reshape(B, S, 3, n_cfg * DIL_HEADS, HEAD_DIM)
    q = apply_rope(qkv[:, :, 0], cos, sin)
    k = apply_rope(qkv[:, :, 1], cos, sin)
    v = qkv[:, :, 2]
    outs, lses = [], []
    for g, (window, dilation) in enumerate(DIL_CONFIGS):
        sl = slice(g * DIL_HEADS, (g + 1) * DIL_HEADS)
        o, lse = dilated_window_attention(q[:, :, sl], k[:, :, sl], v[:, :, sl], window, dilation)
        outs.append(o)
        lses.append(lse)
    wts = jax.nn.softmax(jnp.stack(lses, 0), axis=0)
    o = jnp.sum(wts[..., None] * jnp.stack(outs, 0).astype(jnp.float32), 0).astype(qkv.dtype)
    return o.reshape(B, S, DIL_HEADS * HEAD_DIM)


def hybrid_mixer(x, tabs, w_in, q_norm, kv_norm, w_uq, w_ukv, w_o):
    (cos_m, sin_m), (cos_h, sin_h), (cos_i, sin_i) = tabs
    u = jnp.einsum('bsd,dc->bsc', x, w_in)
    cuts = [int(c) for c in np.cumsum(IN_SPLITS)[:-1]]
    a_cq, a_ckv, a_kpe, b_q, b_k, b_v, b_qi, b_ki, b_wi, c_qkv = jnp.split(u, cuts, axis=-1)
    o_a = mla_mixer(a_cq, a_ckv, a_kpe, q_norm, kv_norm, w_uq, w_ukv, cos_m, sin_m)
    o_b = dsa_mixer(b_q, b_k, b_v, b_qi, b_ki, b_wi, cos_h, sin_h, cos_i, sin_i)
    o_c = dilated_mixer(c_qkv, cos_h, sin_h)
    o = jnp.concatenate([o_a, o_b, o_c], -1)
    return jnp.einsum('bsc,cd->bsd', o, w_o)


def hier_moe(x, rg_w, rg_b, re_w, re_b, w_gate, w_up, w_down):
    B, S, D = x.shape
    t = x.reshape(B * S, D)
    lg = jnp.einsum('td,dg->tg', t, rg_w, preferred_element_type=jnp.float32) + rg_b.astype(jnp.float32)
    g_star = jnp.argmax(lg, -1)
    g_hot = jax.nn.one_hot(g_star, N_GROUPS, dtype=jnp.float32)
    p_top = jnp.sum(jax.nn.softmax(lg, -1) * g_hot, -1, keepdims=True)
    le = jnp.einsum('td,gde->tge', t, re_w, preferred_element_type=jnp.float32) + re_b.astype(jnp.float32)
    le = jnp.einsum('tge,tg->te', le, g_hot)
    top_v, top_i = lax.top_k(le, TOP_K_IN_GROUP)
    w_sel = jax.nn.softmax(top_v, -1)
    gate_e = jnp.sum(jax.nn.one_hot(top_i, EXPERTS_PER_GROUP, dtype=jnp.float32) * w_sel[..., None], 1)
    gate = (g_hot[:, :, None] * (p_top * gate_e)[:, None, :]).astype(x.dtype)
    y = jnp.zeros_like(t)
    for g in range(N_GROUPS):
        h = jax.nn.silu(jnp.einsum('td,edf->tef', t, w_gate[g])) * jnp.einsum('td,edf->tef', t, w_up[g])
        y = y + jnp.einsum('tef,efd->td', h * gate[:, g, :, None], w_down[g])
    return y.reshape(B, S, D)


def setup_inputs(seed: int = 0) -> dict:
    key = jax.random.key(seed)
    ks = jax.random.split(key, 20)

    def nrm(k, shape, scale):
        return jax.random.normal(k, shape, jnp.float32) * scale

    x = nrm(ks[0], (BATCH, SEQ, D_MODEL), 1.0)
    positions = (jax.random.randint(ks[1], (BATCH, 1), 0, 4096, dtype=jnp.int32)
                 + jnp.arange(SEQ, dtype=jnp.int32)[None, :])
    G, E, F = N_GROUPS, EXPERTS_PER_GROUP, EXPERT_HIDDEN
    return {
        "x": x,
        "positions": positions,
        "w_in": nrm(ks[2], (DEPTH, D_MODEL, IN_COLS), D_MODEL ** -0.5),
        "mla_q_norm": 1.0 + nrm(ks[3], (DEPTH, MLA_Q_RANK), 0.02),
        "mla_kv_norm": 1.0 + nrm(ks[4], (DEPTH, MLA_KV_RANK), 0.02),
        "mla_w_uq": nrm(ks[5], (DEPTH, MLA_Q_RANK, MLA_HEADS * (MLA_NOPE + MLA_ROPE)), MLA_Q_RANK ** -0.5),
        "mla_w_ukv": nrm(ks[6], (DEPTH, MLA_KV_RANK, MLA_HEADS * (MLA_NOPE + MLA_V)), MLA_KV_RANK ** -0.5),
        "w_o": nrm(ks[7], (DEPTH, MIX_WIDTH, D_MODEL), MIX_WIDTH ** -0.5 * DEEPNORM_BETA),
        "ln1_g": 1.0 + nrm(ks[8], (DEPTH, D_MODEL), 0.02),
        "ln1_b": nrm(ks[9], (DEPTH, D_MODEL), 0.02),
        "router_group_w": nrm(ks[10], (DEPTH, D_MODEL, G), D_MODEL ** -0.5),
        "router_group_b": nrm(ks[11], (DEPTH, G), 0.01),
        "router_expert_w": nrm(ks[12], (DEPTH, G, D_MODEL, E), D_MODEL ** -0.5),
        "router_expert_b": nrm(ks[13], (DEPTH, G, E), 0.01),
        "expert_w_gate": nrm(ks[14], (DEPTH, G, E, D_MODEL, F), D_MODEL ** -0.5),
        "expert_w_up": nrm(ks[15], (DEPTH, G, E, D_MODEL, F), D_MODEL ** -0.5),
        "expert_w_down": nrm(ks[16], (DEPTH, G, E, F, D_MODEL), F ** -0.5 * DEEPNORM_BETA),
        "ln2_g": 1.0 + nrm(ks[17], (DEPTH, D_MODEL), 0.02),
        "ln2_b": nrm(ks[18], (DEPTH, D_MODEL), 0.02),
    }


def reference(x, positions, w_in, mla_q_norm, mla_kv_norm, mla_w_uq, mla_w_ukv, w_o, ln1_g, ln1_b,
              router_group_w, router_group_b, router_expert_w, router_expert_b,
              expert_w_gate, expert_w_up, expert_w_down, ln2_g, ln2_b):
    tabs = (rope_table(positions, MLA_ROPE),
            rope_table(positions, ROT_HEAD_DIMS),
            rope_table(positions, ROT_IDX_DIMS))
    for l in range(DEPTH):
        mix = hybrid_mixer(x, tabs, w_in[l], mla_q_norm[l], mla_kv_norm[l], mla_w_uq[l], mla_w_ukv[l], w_o[l])
        x = layer_norm(DEEPNORM_ALPHA * x + mix, ln1_g[l], ln1_b[l])
        moe = hier_moe(x, router_group_w[l], router_group_b[l], router_expert_w[l], router_expert_b[l],
                       expert_w_gate[l], expert_w_up[l], expert_w_down[l])
        x = layer_norm(DEEPNORM_ALPHA * x + moe, ln2_g[l], ln2_b[l])
    return x
```

```python
import functools

import numpy as np
import jax
import jax.numpy as jnp
from jax import lax
from jax.experimental import pallas as pl
from jax.experimental.pallas import tpu as pltpu

D_MODEL = 1024
DEPTH = 4
HEAD_DIM = 64
MLA_HEADS = 8
MLA_Q_RANK = 256
MLA_KV_RANK = 128
MLA_NOPE = 64
MLA_ROPE = 32
MLA_V = 64
DSA_HEADS = 4
IDX_HEADS = 8
IDX_DIM = 32
DSA_TOPK_MAX = 256
DIL_CONFIGS = ((128, 1), (512, 4), (2048, 16))
DIL_HEADS = 4
ROPE_THETA = 500000.0
ROT_HEAD_DIMS = HEAD_DIM // 4
ROT_IDX_DIMS = IDX_DIM // 4
N_GROUPS = 4
EXPERTS_PER_GROUP = 8
N_EXPERTS = N_GROUPS * EXPERTS_PER_GROUP
EXPERT_HIDDEN = 256
DEEPNORM_ALPHA = (2.0 * DEPTH) ** 0.25
LN_EPS = 1e-5
RMS_EPS = 1e-6
NEG_INF = -1e30

LANES = 128
VMEM_CAP_BYTES = 60000 * 1024
INT_MIN = -(2 ** 31)

BF16 = jnp.bfloat16
F32 = jnp.float32

_OFF_CQ = 0
_OFF_CKV = _OFF_CQ + MLA_Q_RANK
_OFF_KPE = _OFF_CKV + MLA_KV_RANK
_OFF_BQ = _OFF_KPE + MLA_ROPE
_OFF_BK = _OFF_BQ + DSA_HEADS * HEAD_DIM
_OFF_BV = _OFF_BK + HEAD_DIM
_OFF_BQI = _OFF_BV + HEAD_DIM
_OFF_BKI = _OFF_BQI + IDX_HEADS * IDX_DIM
_OFF_BWI = _OFF_BKI + IDX_DIM
_OFF_C = _OFF_BWI + IDX_HEADS
_N_DIL = len(DIL_CONFIGS) * DIL_HEADS * HEAD_DIM
_IN_COLS = _OFF_C + 3 * _N_DIL

_WA = MLA_Q_RANK + MLA_KV_RANK + LANES
_WB = 5 * 256 + LANES
_WC = 3 * _N_DIL


def _vmem_limit(est_bytes):
    return int(min(max(2 * est_bytes, 32 * 1024 * 1024), VMEM_CAP_BYTES))


def _dot(a, b):
    return jnp.dot(a, b, preferred_element_type=F32)


def _dot_nt(a, b):
    return lax.dot_general(a, b, (((1,), (1,)), ((), ())), preferred_element_type=F32)


def _inproj_column_map():
    a = list(range(_OFF_CQ, _OFF_CQ + MLA_Q_RANK)) + list(range(_OFF_CKV, _OFF_CKV + MLA_KV_RANK))
    kpe = [-1] * LANES
    kpe[MLA_NOPE:MLA_NOPE + MLA_ROPE] = range(_OFF_KPE, _OFF_KPE + MLA_ROPE)
    a += kpe
    b = list(range(_OFF_BQ, _OFF_BQ + 256))
    b += list(range(_OFF_BK, _OFF_BK + HEAD_DIM)) * DSA_HEADS
    b += list(range(_OFF_BV, _OFF_BV + HEAD_DIM)) * DSA_HEADS
    b += list(range(_OFF_BQI, _OFF_BQI + 256))
    b += list(range(_OFF_BKI, _OFF_BKI + IDX_DIM)) * IDX_HEADS
    b += list(range(_OFF_BWI, _OFF_BWI + IDX_HEADS)) + [-1] * (LANES - IDX_HEADS)
    c = list(range(_OFF_C, _OFF_C + _WC))
    assert len(a) == _WA and len(b) == _WB and len(c) == _WC
    return np.asarray(a + b + c, np.int32)


def _gather_cols(w, cols):
    cols = np.asarray(cols, np.int32)
    g = jnp.take(w, jnp.asarray(np.maximum(cols, 0)), axis=-1)
    return jnp.where(jnp.asarray(cols >= 0), g, 0.0)


def _uq_column_map():
    cols = []
    for h in range(MLA_HEADS):
        base = h * (MLA_NOPE + MLA_ROPE)
        cols += list(range(base, base + MLA_NOPE + MLA_ROPE)) + [-1] * (LANES - MLA_NOPE - MLA_ROPE)
    return cols


def _ukv_column_map():
    kcols, vcols = [], []
    for h in range(MLA_HEADS):
        base = h * (MLA_NOPE + MLA_V)
        kcols += list(range(base, base + MLA_NOPE)) + [-1] * (LANES - MLA_NOPE)
        vcols += list(range(base + MLA_NOPE, base + MLA_NOPE + MLA_V))
    return kcols + vcols


def _rope_tables(positions):
    pos = positions.astype(F32)[..., None]

    def cs(rot_dims):
        inv = ROPE_THETA ** (-jnp.arange(0, rot_dims, 2, dtype=F32) / rot_dims)
        ang = pos * inv
        return jnp.cos(ang), jnp.sin(ang)

    def pattern(c, s, period, start):
        half = c.shape[-1]
        one = jnp.ones(c.shape[:-1] + (period,), F32)
        zero = jnp.zeros_like(one)
        cc = one.at[..., start:start + half].set(c).at[..., start + half:start + 2 * half].set(c)
        ss = zero.at[..., start:start + half].set(-s).at[..., start + half:start + 2 * half].set(s)
        reps = LANES // period
        return jnp.tile(cc, (1, 1, reps)), jnp.tile(ss, (1, 1, reps))

    cm, sm = pattern(*cs(MLA_ROPE), LANES, MLA_NOPE)
    ch, sh = pattern(*cs(ROT_HEAD_DIMS), HEAD_DIM, 0)
    ci, si = pattern(*cs(ROT_IDX_DIMS), IDX_DIM, 0)
    return cm, sm, ch, sh, ci, si


def _rope_block(u, c, s, first_half, half):
    rot = jnp.where(first_half, pltpu.roll(u, LANES - half, 1), pltpu.roll(u, half, 1))
    return u * c + rot * s


def _inproj_kernel(x_ref, wa_ref, wb_ref, wc_ref, wuq_ref, wukv_ref, qn_ref, kvn_ref,
                   cm_ref, sm_ref, ch_ref, sh_ref, ci_ref, si_ref,
                   mq_ref, mk_ref, mv_ref, dq_ref, dk_ref, dv_ref, qi_ref, ki_ref, wi_ref,
                   cq_ref, ck_ref, cv_ref):
    tm = x_ref.shape[0]
    lane = lax.broadcasted_iota(jnp.int32, (tm, LANES), 1)
    first_m = (lane >= MLA_NOPE) & (lane < MLA_NOPE + MLA_ROPE // 2)
    first_h = (lane & (HEAD_DIM - 1)) < ROT_HEAD_DIMS // 2
    first_i = (lane & (IDX_DIM - 1)) < ROT_IDX_DIMS // 2
    cm, sm = cm_ref[...], sm_ref[...]
    ch, sh = ch_ref[...], sh_ref[...]
    ci, si = ci_ref[...], si_ref[...]
    rope_m = functools.partial(_rope_block, c=cm, s=sm, first_half=first_m, half=MLA_ROPE // 2)
    rope_h = functools.partial(_rope_block, c=ch, s=sh, first_half=first_h, half=ROT_HEAD_DIMS // 2)
    rope_i = functools.partial(_rope_block, c=ci, s=si, first_half=first_i, half=ROT_IDX_DIMS // 2)

    xb = x_ref[...].astype(BF16)

    ua = _dot(xb, wa_ref[0])
    cq = ua[:, :MLA_Q_RANK]
    cqn = cq * lax.rsqrt(jnp.mean(cq * cq, axis=-1, keepdims=True) + RMS_EPS) * qn_ref[0]
    q = _dot(cqn.astype(BF16), wuq_ref[0])
    mla_scale = (MLA_NOPE + MLA_ROPE) ** -0.5
    for h in range(MLA_HEADS):
        sl = slice(h * LANES, (h + 1) * LANES)
        mq_ref[:, sl] = (rope_m(q[:, sl]) * mla_scale).astype(BF16)
    ckv = ua[:, MLA_Q_RANK:MLA_Q_RANK + MLA_KV_RANK]
    ckvn = ckv * lax.rsqrt(jnp.mean(ckv * ckv, axis=-1, keepdims=True) + RMS_EPS) * kvn_ref[0]
    kv = _dot(ckvn.astype(BF16), wukv_ref[0])
    kpe = rope_m(ua[:, MLA_Q_RANK + MLA_KV_RANK:])
    for h in range(MLA_HEADS):
        sl = slice(h * LANES, (h + 1) * LANES)
        mk_ref[:, sl] = (kv[:, sl] + kpe).astype(BF16)
    mv_ref[...] = kv[:, MLA_HEADS * LANES:].astype(BF16)

    ub = _dot(xb, wb_ref[0])
    dsa_scale = HEAD_DIM ** -0.5
    idx_scale = IDX_DIM ** -0.5 * IDX_HEADS ** -0.5
    for j in range(2):
        sl = slice(j * LANES, (j + 1) * LANES)
        dq_ref[:, sl] = (rope_h(ub[:, sl]) * dsa_scale).astype(BF16)
        dk_ref[:, sl] = rope_h(ub[:, 256 + j * LANES:256 + (j + 1) * LANES]).astype(BF16)
        dv_ref[:, sl] = ub[:, 512 + j * LANES:512 + (j + 1) * LANES].astype(BF16)
        qi_ref[:, sl] = rope_i(ub[:, 768 + j * LANES:768 + (j + 1) * LANES]).astype(BF16)
        ki_ref[:, sl] = rope_i(ub[:, 1024 + j * LANES:1024 + (j + 1) * LANES]).astype(BF16)
    wi_ref[...] = ub[:, 1280:] * idx_scale

    uc = _dot(xb, wc_ref[0])
    for j in range(_N_DIL // LANES):
        sl = slice(j * LANES, (j + 1) * LANES)
        cq_ref[:, sl] = (rope_h(uc[:, sl]) * dsa_scale).astype(BF16)
        ck_ref[:, sl] = rope_h(uc[:, _N_DIL + j * LANES:_N_DIL + (j + 1) * LANES]).astype(BF16)
    cv_ref[...] = uc[:, 2 * _N_DIL:].astype(BF16)


def _inproj(layer, x2d, w, tabs, tm):
    t = x2d.shape[0]
    row = lambda n: pl.BlockSpec((tm, n), lambda i: (i, 0))
    wspec = lambda a: pl.BlockSpec((1,) + a.shape[1:], lambda i: (layer,) + (0,) * (a.ndim - 1))
    outs = [(8 * LANES, BF16), (8 * LANES, BF16), (MLA_HEADS * MLA_V, BF16),
            (256, BF16), (256, BF16), (256, BF16), (256, BF16), (256, BF16), (LANES, F32),
            (_N_DIL, BF16), (_N_DIL, BF16), (_N_DIL, BF16)]
    weights = [w["wa"], w["wb"], w["wc"], w["wuq"], w["wukv"], w["qn"], w["kvn"]]
    est = (sum(int(np.prod(a.shape[1:])) * a.dtype.itemsize for a in weights) * 2
           + tm * D_MODEL * 4 * 2 + 6 * tm * LANES * 4 * 2
           + sum(tm * n * jnp.dtype(d).itemsize for n, d in outs) * 2
           + tm * (_WA + _WB + _WC + 8 * LANES + 12 * LANES) * 4)
    return pl.pallas_call(
        _inproj_kernel,
        grid=(t // tm,),
        in_specs=[row(D_MODEL)] + [wspec(a) for a in weights] + [row(LANES)] * 6,
        out_specs=[row(n) for n, _ in outs],
        out_shape=[jax.ShapeDtypeStruct((t, n), d) for n, d in outs],
        compiler_params=pltpu.CompilerParams(dimension_semantics=("arbitrary",),
                                             vmem_limit_bytes=_vmem_limit(est)),
        name="inproj",
    )(x2d, *weights, *tabs)


def _mla_kernel(q_ref, k_ref, v_ref, o_ref, m_scr, l_scr, acc_scr, *, tq):
    qi = pl.program_id(2)
    lane = lax.broadcasted_iota(jnp.int32, (tq, LANES), 1)
    row = lax.broadcasted_iota(jnp.int32, (tq, tq), 0)
    col = lax.broadcasted_iota(jnp.int32, (tq, tq), 1)
    out = jnp.zeros((tq, LANES), F32)
    for hh in range(2):
        hs = slice(hh * LANES, (hh + 1) * LANES)
        q = q_ref[0, :, hs]
        m_scr[...] = jnp.full((tq, 1), NEG_INF, F32)
        l_scr[...] = jnp.zeros((tq, 1), F32)
        acc_scr[...] = jnp.zeros((tq, LANES), F32)

        def step(j, diagonal):
            start = pl.multiple_of(j * tq, tq)
            k = k_ref[0, pl.ds(start, tq), hs]
            v = v_ref[0, pl.ds(start, tq), :]
            s = _dot_nt(q, k)
            if diagonal:
                s = jnp.where(col <= row, s, NEG_INF)
            m_prev = m_scr[...]
            m_new = jnp.maximum(m_prev, jnp.max(s, axis=-1, keepdims=True))
            alpha = jnp.exp(m_prev - m_new)
            p = jnp.exp(s - m_new)
            l_scr[...] = alpha * l_scr[...] + jnp.sum(p, axis=-1, keepdims=True)
            acc_scr[...] = alpha * acc_scr[...] + _dot(p.astype(BF16), v)
            m_scr[...] = m_new

        def body(j, carry):
            step(j, False)
            return carry

        lax.fori_loop(0, qi, body, 0)
        step(qi, True)
        o = acc_scr[...] / l_scr[...]
        out = jnp.where((lane >= hh * MLA_V) & (lane < (hh + 1) * MLA_V), o, out)
    o_ref[0] = out.astype(BF16)


def _mla_attention(mq, mk, mv, b, s, tq):
    est = tq * 256 * 2 * 2 + s * 256 * 2 * 2 + s * LANES * 2 * 2 + tq * LANES * 2 * 2 + 6 * tq * tq * 4
    return pl.pallas_call(
        functools.partial(_mla_kernel, tq=tq),
        grid=(b, MLA_HEADS // 2, s // tq),
        in_specs=[pl.BlockSpec((1, tq, 256), lambda bi, h, i: (bi, i, h)),
                  pl.BlockSpec((1, s, 256), lambda bi, h, i: (bi, 0, h)),
                  pl.BlockSpec((1, s, LANES), lambda bi, h, i: (bi, 0, h))],
        out_specs=pl.BlockSpec((1, tq, LANES), lambda bi, h, i: (bi, i, h)),
        out_shape=jax.ShapeDtypeStruct((b, s, MLA_HEADS * MLA_V), BF16),
        scratch_shapes=[pltpu.VMEM((tq, 1), F32), pltpu.VMEM((tq, 1), F32), pltpu.VMEM((tq, LANES), F32)],
        compiler_params=pltpu.CompilerParams(dimension_semantics=("arbitrary",) * 3,
                                             vmem_limit_bytes=_vmem_limit(est)),
        name="mla_attn",
    )(mq.reshape(b, s, -1), mk.reshape(b, s, -1), mv.reshape(b, s, -1))


def _dsa_kernel(q_ref, k_ref, v_ref, qi_ref, ki_ref, wi_ref, o_ref, key_scr, sel_scr, *, tq, s_len, top_k):
    q0 = pl.program_id(1) * tq
    lane = lax.broadcasted_iota(jnp.int32, (tq, 256), 1)
    qpos = q0 + lax.broadcasted_iota(jnp.int32, (tq, s_len), 0)
    kpos = lax.broadcasted_iota(jnp.int32, (tq, s_len), 1)
    causal = kpos <= qpos

    qi_all = qi_ref[0]
    ki = ki_ref[0]
    wi = wi_ref[0]
    score = jnp.zeros((tq, s_len), F32)
    for h in range(IDX_HEADS):
        qm = jnp.where((lane >> 5) == h, qi_all, jnp.zeros_like(qi_all))
        score = score + jnp.maximum(_dot_nt(qm, ki), 0.0) * wi[:, h:h + 1]
    score = jnp.where(causal, score, NEG_INF) + 0.0

    bits = lax.bitcast_convert_type(score, jnp.int32)
    key_scr[...] = bits ^ ((bits >> 31) & jnp.int32(0x7FFFFFFF))
    kf = jnp.float32(top_k)

    def count_ge(t):
        return jnp.sum(jnp.where(key_scr[...] >= t, 1.0, 0.0), axis=-1, keepdims=True)

    t0 = jnp.where(count_ge(jnp.zeros((tq, 1), jnp.int32)) >= kf, jnp.int32(0), jnp.int32(INT_MIN))

    def bisect(b, t):
        cand = t | lax.shift_left(jnp.int32(1), 30 - b)
        return jnp.where(count_ge(cand) >= kf, cand, t)

    thr = lax.fori_loop(0, 31, bisect, t0)

    key = key_scr[...]
    gt = key > thr
    eq = key == thr
    n_gt = jnp.sum(jnp.where(gt, 1.0, 0.0), axis=-1, keepdims=True)
    n_eq = jnp.sum(jnp.where(eq, 1.0, 0.0), axis=-1, keepdims=True)
    n_eq_valid = jnp.sum(jnp.where(eq & causal, 1.0, 0.0), axis=-1, keepdims=True)
    room = kf - n_gt
    sel_scr[...] = jnp.where(gt | eq, 1.0, 0.0)
    need_rank = jnp.max(jnp.where((n_eq > room) & (n_eq_valid > 0.0), 1.0, 0.0)) > 0.0

    @pl.when(need_rank)
    def _():
        ch = 256 if s_len % 256 == 0 else s_len
        r_i = lax.broadcasted_iota(jnp.int32, (ch, ch), 0)
        c_i = lax.broadcasted_iota(jnp.int32, (ch, ch), 1)
        upper = jnp.where(r_i < c_i, 1.0, 0.0).astype(BF16)
        before = jnp.zeros((tq, 1), F32)
        for c in range(s_len // ch):
            cs = slice(c * ch, (c + 1) * ch)
            key_c = key_scr[:, cs]
            eq_c = jnp.where(key_c == thr, 1.0, 0.0)
            rank = _dot(eq_c.astype(BF16), upper) + before
            keep = (key_c > thr) | ((key_c == thr) & (rank < room))
            sel_scr[:, cs] = jnp.where(keep, 1.0, 0.0)
            before = before + jnp.sum(eq_c, axis=-1, keepdims=True)

    selected = (sel_scr[...] > 0.0) & causal
    q_all = q_ref[0]
    k4 = k_ref[0]
    v4 = v_ref[0]
    out = jnp.zeros((tq, 256), F32)
    for h in range(DSA_HEADS):
        head = (lane >> 6) == h
        qm = jnp.where(head, q_all, jnp.zeros_like(q_all))
        sc = jnp.where(selected, _dot_nt(qm, k4), NEG_INF)
        m = jnp.max(sc, axis=-1, keepdims=True)
        p = jnp.exp(sc - m)
        l = jnp.sum(p, axis=-1, keepdims=True)
        o = _dot(p.astype(BF16), v4)
        out = jnp.where(head, o / l, out)
    o_ref[0] = out.astype(BF16)


def _dsa_attention(dq, dk, dv, qi, ki, wi, b, s, tq):
    top_k = min(DSA_TOPK_MAX, s // 4)
    qspec = lambda n: pl.BlockSpec((1, tq, n), lambda bi, i: (bi, i, 0))
    kspec = lambda n: pl.BlockSpec((1, s, n), lambda bi, i: (bi, 0, 0))
    est = 3 * tq * 256 * 2 * 2 + 3 * s * 256 * 2 * 2 + tq * LANES * 4 * 2 + 8 * tq * s * 4
    r3 = lambda a: a.reshape(b, s, -1)
    return pl.pallas_call(
        functools.partial(_dsa_kernel, tq=tq, s_len=s, top_k=top_k),
        grid=(b, s // tq),
        in_specs=[qspec(256), kspec(256), kspec(256), qspec(256), kspec(256), qspec(LANES)],
        out_specs=qspec(256),
        out_shape=jax.ShapeDtypeStruct((b, s, 256), BF16),
        scratch_shapes=[pltpu.VMEM((tq, s), jnp.int32), pltpu.VMEM((tq, s), F32)],
        compiler_params=pltpu.CompilerParams(dimension_semantics=("arbitrary",) * 2,
                                             vmem_limit_bytes=_vmem_limit(est)),
        name="dsa_attn",
    )(r3(dq), r3(dk), r3(dv), r3(qi), r3(ki), r3(wi))


def _dilated_kernel(q_ref, kp_ref, kc_ref, vp_ref, vc_ref, o_ref, lse_ref, *, w):
    j = pl.program_id(2)
    lane = lax.broadcasted_iota(jnp.int32, (w, 256), 1)
    row = lax.broadcasted_iota(jnp.int32, (w, 2 * w), 0)
    col = lax.broadcasted_iota(jnp.int32, (w, 2 * w), 1)
    rel = row + w - col
    ok = (rel >= 0) & (rel <= w) & ((col >= w) | (j > 0))
    q_all = q_ref[0]
    k = jnp.concatenate([kp_ref[0], kc_ref[0]], axis=0)
    v = jnp.concatenate([vp_ref[0], vc_ref[0]], axis=0)
    out = jnp.zeros((w, 256), F32)
    lse = jnp.zeros((w, 256), F32)
    for h in range(DIL_HEADS):
        head = (lane >> 6) == h
        qm = jnp.where(head, q_all, jnp.zeros_like(q_all))
        sc = jnp.where(ok, _dot_nt(qm, k), NEG_INF)
        m = jnp.max(sc, axis=-1, keepdims=True)
        p = jnp.exp(sc - m)
        l = jnp.sum(p, axis=-1, keepdims=True)
        o = _dot(p.astype(BF16), v)
        out = jnp.where(head, o / l, out)
        lse = jnp.where(head, m + jnp.log(l), lse)
    o_ref[0] = out
    lse_ref[0] = lse


def _dilated_attention(cq, ck, cv, g, b, s):
    window, d = DIL_CONFIGS[g]
    w = window // d
    n_cfg = len(DIL_CONFIGS)
    steps = s // d
    assert steps % w == 0
    view = lambda a: a.reshape(b, steps, d * a.shape[-1])
    cur = pl.BlockSpec((1, w, 256), lambda bi, r, j: (bi, j, r * n_cfg + g))
    prev = pl.BlockSpec((1, w, 256), lambda bi, r, j: (bi, jnp.maximum(j - 1, 0), r * n_cfg + g))
    ospec = pl.BlockSpec((1, w, 256), lambda bi, r, j: (bi, j, r))
    est = 5 * w * 256 * 2 * 2 + 2 * w * 256 * 4 * 2 + 8 * w * 2 * w * 4
    o, lse = pl.pallas_call(
        functools.partial(_dilated_kernel, w=w),
        grid=(b, d, steps // w),
        in_specs=[cur, prev, cur, prev, cur],
        out_specs=[ospec, ospec],
        out_shape=[jax.ShapeDtypeStruct((b, steps, d * 256), F32)] * 2,
        compiler_params=pltpu.CompilerParams(dimension_semantics=("arbitrary",) * 3,
                                             vmem_limit_bytes=_vmem_limit(est)),
        name=f"dilated_attn_{g}",
    )(view(cq), view(ck), view(ck), view(cv), view(cv))
    return o.reshape(b * s, 256), lse.reshape(b * s, 256)


def _layer_norm(y, g, b):
    mu = jnp.mean(y, axis=-1, keepdims=True)
    yc = y - mu
    var = jnp.mean(yc * yc, axis=-1, keepdims=True)
    return yc * lax.rsqrt(var + LN_EPS) * g + b


def _outproj_kernel(x_ref, oa_ref, ob_ref, o0_ref, o1_ref, o2_ref, l0_ref, l1_ref, l2_ref,
                    wo_ref, g_ref, b_ref, rwh_ref, rwl_ref, rb_ref, x1_ref, gate_ref):
    tm = x_ref.shape[0]
    l0, l1, l2 = l0_ref[...], l1_ref[...], l2_ref[...]
    mx = jnp.maximum(jnp.maximum(l0, l1), l2)
    e0, e1, e2 = jnp.exp(l0 - mx), jnp.exp(l1 - mx), jnp.exp(l2 - mx)
    oc = (e0 * o0_ref[...] + e1 * o1_ref[...] + e2 * o2_ref[...]) / (e0 + e1 + e2)
    na = MLA_HEADS * MLA_V
    mix = (_dot(oa_ref[...], wo_ref[0, :na, :]) + _dot(ob_ref[...], wo_ref[0, na:na + 256, :])
           + _dot(oc.astype(BF16), wo_ref[0, na + 256:, :]))
    x1 = _layer_norm(DEEPNORM_ALPHA * x_ref[...] + mix, g_ref[0], b_ref[0])
    x1_ref[...] = x1

    xh = x1.astype(BF16)
    xl = (x1 - xh.astype(F32)).astype(BF16)
    logits = _dot(xh, rwh_ref[0]) + _dot(xl, rwh_ref[0]) + _dot(xh, rwl_ref[0]) + rb_ref[0]
    lane = lax.broadcasted_iota(jnp.int32, (tm, LANES), 1)
    big = jnp.int32(LANES)
    is_group = (lane >= N_EXPERTS) & (lane < N_EXPERTS + N_GROUPS)
    gl = jnp.where(is_group, logits, -jnp.inf)
    gmax = jnp.max(gl, axis=-1, keepdims=True)
    g_star = jnp.min(jnp.where(gl == gmax, lane - N_EXPERTS, big), axis=-1, keepdims=True)
    p_top = 1.0 / jnp.sum(jnp.exp(gl - gmax), axis=-1, keepdims=True)
    in_group = (lane < N_EXPERTS) & ((lane >> 3) == g_star)
    el = jnp.where(in_group, logits, -jnp.inf)
    v1 = jnp.max(el, axis=-1, keepdims=True)
    i1 = jnp.min(jnp.where(el == v1, lane, big), axis=-1, keepdims=True)
    el2 = jnp.where(lane == i1, -jnp.inf, el)
    v2 = jnp.max(el2, axis=-1, keepdims=True)
    i2 = jnp.min(jnp.where(el2 == v2, lane, big), axis=-1, keepdims=True)
    e21 = jnp.exp(v2 - v1)
    w1 = 1.0 / (1.0 + e21)
    w2 = e21 / (1.0 + e21)
    gate_ref[...] = p_top * (jnp.where(lane == i1, w1, 0.0) + jnp.where(lane == i2, w2, 0.0))


def _outproj(layer, x2d, oa, ob, dil, w, tm):
    t = x2d.shape[0]
    row = lambda n: pl.BlockSpec((tm, n), lambda i: (i, 0))
    wspec = lambda a: pl.BlockSpec((1,) + a.shape[1:], lambda i: (layer,) + (0,) * (a.ndim - 1))
    weights = [w["wo"], w["ln1_g"], w["ln1_b"], w["rwh"], w["rwl"], w["rb"]]
    est = (D_MODEL * D_MODEL * 2 * 2 + 2 * D_MODEL * LANES * 2 * 2 + tm * D_MODEL * 4 * 4
           + tm * (512 + 256) * 2 * 2 + 6 * tm * 256 * 4 * 2 + tm * LANES * 4 * 2 + 6 * tm * D_MODEL * 4)
    return pl.pallas_call(
        _outproj_kernel,
        grid=(t // tm,),
        in_specs=[row(D_MODEL), row(MLA_HEADS * MLA_V), row(256)] + [row(256)] * 6 + [wspec(a) for a in weights],
        out_specs=[row(D_MODEL), row(LANES)],
        out_shape=[jax.ShapeDtypeStruct((t, D_MODEL), F32), jax.ShapeDtypeStruct((t, LANES), F32)],
        compiler_params=pltpu.CompilerParams(dimension_semantics=("arbitrary",),
                                             vmem_limit_bytes=_vmem_limit(est)),
        name="outproj_ln_router",
    )(x2d, oa, ob, dil[0][0], dil[1][0], dil[2][0], dil[0][1], dil[1][1], dil[2][1], *weights)


def _moe_kernel(x_ref, gate_ref, wg_ref, wu_ref, wd_ref, g_ref, b_ref, o_ref, xb_scr, acc_scr):
    e = pl.program_id(1)
    tm = x_ref.shape[0]

    @pl.when(e == 0)
    def _():
        xb_scr[...] = x_ref[...].astype(BF16)
        acc_scr[...] = jnp.zeros_like(acc_scr)

    xb = xb_scr[...]
    a = _dot(xb, wg_ref[0, 0])
    u = _dot(xb, wu_ref[0, 0])
    lane = lax.broadcasted_iota(jnp.int32, (tm, LANES), 1)
    ge = jnp.sum(jnp.where(lane == e, gate_ref[...], 0.0), axis=-1, keepdims=True)
    h = (a / (1.0 + jnp.exp(-a))) * u * ge
    acc_scr[...] += _dot(h.astype(BF16), wd_ref[0, 0])

    @pl.when(e == pl.num_programs(1) - 1)
    def _():
        o_ref[...] = _layer_norm(DEEPNORM_ALPHA * x_ref[...] + acc_scr[...], g_ref[0], b_ref[0])


def _moe(layer, x1, gate, w, tm):
    t = x1.shape[0]
    f = EXPERT_HIDDEN
    est = (tm * D_MODEL * 4 * 4 + tm * LANES * 4 * 2 + 3 * D_MODEL * f * 2 * 2
           + tm * D_MODEL * 6 + 4 * tm * f * 4)
    return pl.pallas_call(
        _moe_kernel,
        grid=(t // tm, N_EXPERTS),
        in_specs=[pl.BlockSpec((tm, D_MODEL), lambda i, e: (i, 0)),
                  pl.BlockSpec((tm, LANES), lambda i, e: (i, 0)),
                  pl.BlockSpec((1, 1, D_MODEL, f), lambda i, e: (layer, e, 0, 0)),
                  pl.BlockSpec((1, 1, D_MODEL, f), lambda i, e: (layer, e, 0, 0)),
                  pl.BlockSpec((1, 1, f, D_MODEL), lambda i, e: (layer, e, 0, 0)),
                  pl.BlockSpec((1, 1, D_MODEL), lambda i, e: (layer, 0, 0)),
                  pl.BlockSpec((1, 1, D_MODEL), lambda i, e: (layer, 0, 0))],
        out_specs=pl.BlockSpec((tm, D_MODEL), lambda i, e: (i, 0)),
        out_shape=jax.ShapeDtypeStruct((t, D_MODEL), F32),
        scratch_shapes=[pltpu.VMEM((tm, D_MODEL), BF16), pltpu.VMEM((tm, D_MODEL), F32)],
        compiler_params=pltpu.CompilerParams(dimension_semantics=("arbitrary", "arbitrary"),
                                             vmem_limit_bytes=_vmem_limit(est)),
        name="moe_ln",
    )(x1, gate, w["wg"], w["wu"], w["wd"], w["ln2_g"], w["ln2_b"])


def _pick_tile(n, pref):
    t = min(pref, n)
    while n % t:
        t //= 2
    return t


def _prepare_weights(w_in, mla_q_norm, mla_kv_norm, mla_w_uq, mla_w_ukv, w_o, ln1_g, ln1_b,
                     router_group_w, router_group_b, router_expert_w, router_expert_b,
                     expert_w_gate, expert_w_up, expert_w_down, ln2_g, ln2_b):
    depth = w_in.shape[0]
    w_perm = _gather_cols(w_in, _inproj_column_map()).astype(BF16)
    vec = lambda a: a.reshape(depth, 1, -1)
    rw = jnp.concatenate([jnp.moveaxis(router_expert_w, 1, 2).reshape(depth, D_MODEL, N_EXPERTS),
                          router_group_w,
                          jnp.zeros((depth, D_MODEL, LANES - N_EXPERTS - N_GROUPS), F32)], axis=-1)
    rb = jnp.concatenate([router_expert_b.reshape(depth, N_EXPERTS), router_group_b,
                          jnp.zeros((depth, LANES - N_EXPERTS - N_GROUPS), F32)], axis=-1)
    rwh = rw.astype(BF16)
    ex = lambda a: a.reshape((depth, N_EXPERTS) + a.shape[3:]).astype(BF16)
    return {
        "wa": w_perm[:, :, :_WA], "wb": w_perm[:, :, _WA:_WA + _WB], "wc": w_perm[:, :, _WA + _WB:],
        "wuq": _gather_cols(mla_w_uq, _uq_column_map()).astype(BF16),
        "wukv": _gather_cols(mla_w_ukv, _ukv_column_map()).astype(BF16),
        "qn": vec(mla_q_norm), "kvn": vec(mla_kv_norm),
        "wo": w_o.astype(BF16), "ln1_g": vec(ln1_g), "ln1_b": vec(ln1_b),
        "rwh": rwh, "rwl": (rw - rwh.astype(F32)).astype(BF16), "rb": vec(rb),
        "wg": ex(expert_w_gate), "wu": ex(expert_w_up), "wd": ex(expert_w_down),
        "ln2_g": vec(ln2_g), "ln2_b": vec(ln2_b),
    }


def kernel(x, positions, w_in, mla_q_norm, mla_kv_norm, mla_w_uq, mla_w_ukv, w_o, ln1_g, ln1_b,
           router_group_w, router_group_b, router_expert_w, router_expert_b,
           expert_w_gate, expert_w_up, expert_w_down, ln2_g, ln2_b):
    b, s, d_model = x.shape
    assert d_model == D_MODEL and w_in.shape[-1] == _IN_COLS
    t = b * s
    w = _prepare_weights(w_in, mla_q_norm, mla_kv_norm, mla_w_uq, mla_w_ukv, w_o, ln1_g, ln1_b,
                         router_group_w, router_group_b, router_expert_w, router_expert_b,
                         expert_w_gate, expert_w_up, expert_w_down, ln2_g, ln2_b)
    tabs = [a.reshape(t, LANES) for a in _rope_tables(positions)]
    tm_proj = _pick_tile(t, 512)
    tm_moe = _pick_tile(t, 1024)
    tq_mla = _pick_tile(s, 512)
    tq_dsa = _pick_tile(s, 256)
    x2d = x.reshape(t, D_MODEL)
    for layer in range(w_in.shape[0]):
        mq, mk, mv, dq, dk, dv, qi, ki, wi, cq, ck, cv = _inproj(layer, x2d, w, tabs, tm_proj)
        oa = _mla_attention(mq, mk, mv, b, s, tq_mla).reshape(t, -1)
        ob = _dsa_attention(dq, dk, dv, qi, ki, wi, b, s, tq_dsa).reshape(t, -1)
        dil = [_dilated_attention(cq, ck, cv, g, b, s) for g in range(len(DIL_CONFIGS))]
        x1, gate = _outproj(layer, x2d, oa, ob, dil, w, tm_proj)
        x2d = _moe(layer, x1, gate, w, tm_moe)
    return x2d.reshape(b, s, D_MODEL)
```

```python
import functools

import numpy as np
import jax
import jax.numpy as jnp
from jax import lax
from jax.experimental import pallas as pl
from jax.experimental.pallas import tpu as pltpu

D_MODEL = 1024
DEPTH = 4
HEAD_DIM = 64
MLA_HEADS = 8
MLA_Q_RANK = 256
MLA_KV_RANK = 128
MLA_NOPE = 64
MLA_ROPE = 32
MLA_V = 64
DSA_HEADS = 4
IDX_HEADS = 8
IDX_DIM = 32
DSA_TOPK_MAX = 256
DIL_CONFIGS = ((128, 1), (512, 4), (2048, 16))
DIL_HEADS = 4
ROPE_THETA = 500000.0
ROT_HEAD_DIMS = HEAD_DIM // 4
ROT_IDX_DIMS = IDX_DIM // 4
N_GROUPS = 4
EXPERTS_PER_GROUP = 8
N_EXPERTS = N_GROUPS * EXPERTS_PER_GROUP
EXPERT_HIDDEN = 256
DEEPNORM_ALPHA = (2.0 * DEPTH) ** 0.25
LN_EPS = 1e-5
RMS_EPS = 1e-6
NEG_INF = -1e30

LANES = 128
VMEM_CAP_BYTES = 60000 * 1024
INT_MIN = -(2 ** 31)

BF16 = jnp.bfloat16
F32 = jnp.float32

_OFF_CQ = 0
_OFF_CKV = _OFF_CQ + MLA_Q_RANK
_OFF_KPE = _OFF_CKV + MLA_KV_RANK
_OFF_BQ = _OFF_KPE + MLA_ROPE
_OFF_BK = _OFF_BQ + DSA_HEADS * HEAD_DIM
_OFF_BV = _OFF_BK + HEAD_DIM
_OFF_BQI = _OFF_BV + HEAD_DIM
_OFF_BKI = _OFF_BQI + IDX_HEADS * IDX_DIM
_OFF_BWI = _OFF_BKI + IDX_DIM
_OFF_C = _OFF_BWI + IDX_HEADS
_N_DIL = len(DIL_CONFIGS) * DIL_HEADS * HEAD_DIM
_IN_COLS = _OFF_C + 3 * _N_DIL

_WA = MLA_Q_RANK + MLA_KV_RANK + LANES
_WB = 5 * 256 + LANES
_WC = 3 * _N_DIL


def _vmem_limit(est_bytes):
    return int(min(max(2 * est_bytes, 32 * 1024 * 1024), VMEM_CAP_BYTES))


def _dot(a, b):
    return jnp.dot(a, b, preferred_element_type=F32)


def _dot_nt(a, b):
    return lax.dot_general(a, b, (((1,), (1,)), ((), ())), preferred_element_type=F32)


def _inproj_column_map():
    a = list(range(_OFF_CQ, _OFF_CQ + MLA_Q_RANK)) + list(range(_OFF_CKV, _OFF_CKV + MLA_KV_RANK))
    kpe = [-1] * LANES
    kpe[MLA_NOPE:MLA_NOPE + MLA_ROPE] = range(_OFF_KPE, _OFF_KPE + MLA_ROPE)
    a += kpe
    b = list(range(_OFF_BQ, _OFF_BQ + 256))
    b += list(range(_OFF_BK, _OFF_BK + HEAD_DIM)) * DSA_HEADS
    b += list(range(_OFF_BV, _OFF_BV + HEAD_DIM)) * DSA_HEADS
    b += list(range(_OFF_BQI, _OFF_BQI + 256))
    b += list(range(_OFF_BKI, _OFF_BKI + IDX_DIM)) * IDX_HEADS
    b += list(range(_OFF_BWI, _OFF_BWI + IDX_HEADS)) + [-1] * (LANES - IDX_HEADS)
    c = list(range(_OFF_C, _OFF_C + _WC))
    assert len(a) == _WA and len(b) == _WB and len(c) == _WC
    return np.asarray(a + b + c, np.int32)


def _gather_cols(w, cols):
    cols = np.asarray(cols, np.int32)
    g = jnp.take(w, jnp.asarray(np.maximum(cols, 0)), axis=-1)
    return jnp.where(jnp.asarray(cols >= 0), g, 0.0)


def _uq_column_map():
    cols = []
    for h in range(MLA_HEADS):
        base = h * (MLA_NOPE + MLA_ROPE)
        cols += list(range(base, base + MLA_NOPE + MLA_ROPE)) + [-1] * (LANES - MLA_NOPE - MLA_ROPE)
    return cols


def _ukv_column_map():
    kcols, vcols = [], []
    for h in range(MLA_HEADS):
        base = h * (MLA_NOPE + MLA_V)
        kcols += list(range(base, base + MLA_NOPE)) + [-1] * (LANES - MLA_NOPE)
        vcols += list(range(base + MLA_NOPE, base + MLA_NOPE + MLA_V))
    return kcols + vcols


def _rope_tables(positions):
    pos = positions.astype(F32)[..., None]

    def cs(rot_dims):
        inv = ROPE_THETA ** (-jnp.arange(0, rot_dims, 2, dtype=F32) / rot_dims)
        ang = pos * inv
        return jnp.cos(ang), jnp.sin(ang)

    def pattern(c, s, period, start):
        half = c.shape[-1]
        one = jnp.ones(c.shape[:-1] + (period,), F32)
        zero = jnp.zeros_like(one)
        cc = one.at[..., start:start + half].set(c).at[..., start + half:start + 2 * half].set(c)
        ss = zero.at[..., start:start + half].set(-s).at[..., start + half:start + 2 * half].set(s)
        reps = LANES // period
        return jnp.tile(cc, (1, 1, reps)), jnp.tile(ss, (1, 1, reps))

    cm, sm = pattern(*cs(MLA_ROPE), LANES, MLA_NOPE)
    ch, sh = pattern(*cs(ROT_HEAD_DIMS), HEAD_DIM, 0)
    ci, si = pattern(*cs(ROT_IDX_DIMS), IDX_DIM, 0)
    return cm, sm, ch, sh, ci, si


def _rope_block(u, c, s, first_half, half):
    rot = jnp.where(first_half, pltpu.roll(u, LANES - half, 1), pltpu.roll(u, half, 1))
    return u * c + rot * s


def _inproj_kernel(x_ref, wa_ref, wb_ref, wc_ref, wuq_ref, wukv_ref, qn_ref, kvn_ref,
                   cm_ref, sm_ref, ch_ref, sh_ref, ci_ref, si_ref,
                   mq_ref, mk_ref, mv_ref, dq_ref, dk_ref, dv_ref, qi_ref, ki_ref, wi_ref,
                   *dil_refs_and_scratch):
    dil_refs, perm_scr = dil_refs_and_scratch[:-1], dil_refs_and_scratch[-1]
    tm = x_ref.shape[0]
    lane = lax.broadcasted_iota(jnp.int32, (tm, LANES), 1)
    first_m = (lane >= MLA_NOPE) & (lane < MLA_NOPE + MLA_ROPE // 2)
    first_h = (lane & (HEAD_DIM - 1)) < ROT_HEAD_DIMS // 2
    first_i = (lane & (IDX_DIM - 1)) < ROT_IDX_DIMS // 2
    cm, sm = cm_ref[...], sm_ref[...]
    ch, sh = ch_ref[...], sh_ref[...]
    ci, si = ci_ref[...], si_ref[...]
    rope_m = functools.partial(_rope_block, c=cm, s=sm, first_half=first_m, half=MLA_ROPE // 2)
    rope_h = functools.partial(_rope_block, c=ch, s=sh, first_half=first_h, half=ROT_HEAD_DIMS // 2)
    rope_i = functools.partial(_rope_block, c=ci, s=si, first_half=first_i, half=ROT_IDX_DIMS // 2)

    xb = x_ref[...].astype(BF16)

    ua = _dot(xb, wa_ref[0])
    cq = ua[:, :MLA_Q_RANK]
    cqn = cq * lax.rsqrt(jnp.mean(cq * cq, axis=-1, keepdims=True) + RMS_EPS) * qn_ref[0]
    q = _dot(cqn.astype(BF16), wuq_ref[0])
    mla_scale = (MLA_NOPE + MLA_ROPE) ** -0.5
    for h in range(MLA_HEADS):
        sl = slice(h * LANES, (h + 1) * LANES)
        mq_ref[:, sl] = (rope_m(q[:, sl]) * mla_scale).astype(BF16)
    ckv = ua[:, MLA_Q_RANK:MLA_Q_RANK + MLA_KV_RANK]
    ckvn = ckv * lax.rsqrt(jnp.mean(ckv * ckv, axis=-1, keepdims=True) + RMS_EPS) * kvn_ref[0]
    kv = _dot(ckvn.astype(BF16), wukv_ref[0])
    kpe = rope_m(ua[:, MLA_Q_RANK + MLA_KV_RANK:])
    for h in range(MLA_HEADS):
        sl = slice(h * LANES, (h + 1) * LANES)
        mk_ref[:, sl] = (kv[:, sl] + kpe).astype(BF16)
    mv_ref[...] = kv[:, MLA_HEADS * LANES:].astype(BF16)

    ub = _dot(xb, wb_ref[0])
    dsa_scale = HEAD_DIM ** -0.5
    idx_scale = IDX_DIM ** -0.5 * IDX_HEADS ** -0.5
    for j in range(2):
        sl = slice(j * LANES, (j + 1) * LANES)
        dq_ref[:, sl] = (rope_h(ub[:, sl]) * dsa_scale).astype(BF16)
        dk_ref[:, sl] = rope_h(ub[:, 256 + j * LANES:256 + (j + 1) * LANES]).astype(BF16)
        dv_ref[:, sl] = ub[:, 512 + j * LANES:512 + (j + 1) * LANES].astype(BF16)
        qi_ref[:, sl] = rope_i(ub[:, 768 + j * LANES:768 + (j + 1) * LANES]).astype(BF16)
        ki_ref[:, sl] = rope_i(ub[:, 1024 + j * LANES:1024 + (j + 1) * LANES]).astype(BF16)
    wi_ref[...] = ub[:, 1280:] * idx_scale

    uc = _dot(xb, wc_ref[0])
    n_cfg = len(DIL_CONFIGS)
    for which in range(3):
        for g, (_, d) in enumerate(DIL_CONFIGS):
            halves = []
            for j in range(2):
                c0 = which * _N_DIL + g * 256 + j * LANES
                blk = uc[:, c0:c0 + LANES]
                if which < 2:
                    blk = rope_h(blk)
                if which == 0:
                    blk = blk * dsa_scale
                halves.append(blk)
            out_ref = dil_refs[which * n_cfg + g]
            if d == 1:
                out_ref[:, :LANES] = halves[0].astype(BF16)
                out_ref[:, LANES:] = halves[1].astype(BF16)
            else:
                for j in range(2):
                    slot = 2 * (which * (n_cfg - 1) + g - 1) + j
                    perm_scr[slot] = halves[j]
                    for r in range(d):
                        rows = perm_scr[slot, pl.ds(r, tm // d, stride=d), :]
                        out_ref[0, :, r * 256 + j * LANES:r * 256 + (j + 1) * LANES] = rows.astype(BF16)


def _inproj(layer, x2d, w, tabs, tm, b, s):
    t = x2d.shape[0]
    row = lambda n: pl.BlockSpec((tm, n), lambda i: (i, 0))
    wspec = lambda a: pl.BlockSpec((1,) + a.shape[1:], lambda i: (layer,) + (0,) * (a.ndim - 1))
    outs = [(8 * LANES, BF16), (8 * LANES, BF16), (MLA_HEADS * MLA_V, BF16),
            (256, BF16), (256, BF16), (256, BF16), (256, BF16), (256, BF16), (LANES, F32)]
    weights = [w["wa"], w["wb"], w["wc"], w["wuq"], w["wukv"], w["qn"], w["kvn"]]
    tiles_per_seq = s // tm
    dil_specs, dil_shapes = [], []
    for _ in range(3):
        for _, d in DIL_CONFIGS:
            if d == 1:
                dil_specs.append(row(256))
                dil_shapes.append(jax.ShapeDtypeStruct((t, 256), BF16))
            else:
                dil_specs.append(pl.BlockSpec((1, tm // d, d * 256),
                                              lambda i: (i // tiles_per_seq, i % tiles_per_seq, 0)))
                dil_shapes.append(jax.ShapeDtypeStruct((b, s // d, d * 256), BF16))
    est = (sum(int(np.prod(a.shape[1:])) * a.dtype.itemsize for a in weights) * 2
           + tm * D_MODEL * 4 * 2 + 6 * tm * LANES * 4 * 2
           + (sum(tm * n * jnp.dtype(d).itemsize for n, d in outs) + 9 * tm * 256 * 2) * 2
           + 6 * tm * 256 * 4 + tm * (_WA + _WB + _WC + 8 * LANES + 12 * LANES) * 4)
    res = pl.pallas_call(
        _inproj_kernel,
        grid=(t // tm,),
        in_specs=[row(D_MODEL)] + [wspec(a) for a in weights] + [row(LANES)] * 6,
        out_specs=[row(n) for n, _ in outs] + dil_specs,
        out_shape=[jax.ShapeDtypeStruct((t, n), d) for n, d in outs] + dil_shapes,
        scratch_shapes=[pltpu.VMEM((2 * 3 * (len(DIL_CONFIGS) - 1), tm, LANES), F32)],
        compiler_params=pltpu.CompilerParams(dimension_semantics=("arbitrary",),
                                             vmem_limit_bytes=_vmem_limit(est)),
        name="inproj",
    )(x2d, *weights, *tabs)
    return res[:len(outs)], res[len(outs):]


def _softmax_update(s, v, state):
    m_prev, l_prev, acc_prev = state
    m_new = jnp.maximum(m_prev, jnp.max(s, axis=-1, keepdims=True))
    alpha = jnp.exp(m_prev - m_new)
    p = jnp.exp(s - m_new)
    return (m_new, alpha * l_prev + jnp.sum(p, axis=-1, keepdims=True),
            alpha * acc_prev + _dot(p.astype(BF16), v))


def _mla_kernel(q_ref, k_ref, v_ref, o_ref, *, tq, nh):
    qi = pl.program_id(2)
    lane = lax.broadcasted_iota(jnp.int32, (tq, LANES), 1)
    row = lax.broadcasted_iota(jnp.int32, (tq, tq), 0)
    col = lax.broadcasted_iota(jnp.int32, (tq, tq), 1)

    def step(j, states, diagonal):
        start = pl.multiple_of(j * tq, tq)
        new_states = []
        for h in range(nh):
            hs = slice(h * LANES, (h + 1) * LANES)
            vs = slice((h // 2) * LANES, (h // 2 + 1) * LANES)
            s = _dot_nt(q_ref[0, :, hs], k_ref[0, pl.ds(start, tq), hs])
            if diagonal:
                s = jnp.where(col <= row, s, NEG_INF)
            new_states.append(_softmax_update(s, v_ref[0, pl.ds(start, tq), vs], states[h]))
        return tuple(new_states)

    init = tuple((jnp.full((tq, 1), NEG_INF, F32), jnp.zeros((tq, 1), F32), jnp.zeros((tq, LANES), F32))
                 for _ in range(nh))
    states = lax.fori_loop(0, qi, lambda j, st: step(j, st, False), init)
    states = step(qi, states, True)
    for pair in range(nh // 2):
        (_, l_even, acc_even), (_, l_odd, acc_odd) = states[2 * pair], states[2 * pair + 1]
        o_ref[0, :, pair * LANES:(pair + 1) * LANES] = jnp.where(
            lane < MLA_V, acc_even / l_even, acc_odd / l_odd).astype(BF16)


def _mla_attention(mq, mk, mv, b, s, tq, nh=4):
    est = (tq * nh * LANES * 2 * 2 + s * nh * LANES * 2 * 2 + s * nh * MLA_V * 2 * 2 + tq * nh * MLA_V * 2 * 2
           + nh * tq * 3 * LANES * 4 + 4 * nh * tq * tq * 4)
    return pl.pallas_call(
        functools.partial(_mla_kernel, tq=tq, nh=nh),
        grid=(b, MLA_HEADS // nh, s // tq),
        in_specs=[pl.BlockSpec((1, tq, nh * LANES), lambda bi, h, i: (bi, i, h)),
                  pl.BlockSpec((1, s, nh * LANES), lambda bi, h, i: (bi, 0, h)),
                  pl.BlockSpec((1, s, nh * MLA_V), lambda bi, h, i: (bi, 0, h))],
        out_specs=pl.BlockSpec((1, tq, nh * MLA_V), lambda bi, h, i: (bi, i, h)),
        out_shape=jax.ShapeDtypeStruct((b, s, MLA_HEADS * MLA_V), BF16),
        compiler_params=pltpu.CompilerParams(dimension_semantics=("arbitrary",) * 3,
                                             vmem_limit_bytes=_vmem_limit(est)),
        name="mla_attn",
    )(mq.reshape(b, s, -1), mk.reshape(b, s, -1), mv.reshape(b, s, -1))


def _fold_lanes(x):
    acc = x[:, :LANES]
    for part in range(1, x.shape[1] // LANES):
        acc = acc + x[:, part * LANES:(part + 1) * LANES]
    return acc


def _dsa_kernel(q_ref, k_ref, v_ref, qi_ref, ki_ref, wi_ref, o_ref,
                key_scr, sel_scr, qim_scr, wb_scr, qm_scr, thr_scr, *, tq, top_k):
    i = pl.program_id(1)
    n = i + 1
    lane = lax.broadcasted_iota(jnp.int32, (tq, 256), 1)
    row = lax.broadcasted_iota(jnp.int32, (tq, tq), 0)
    col = lax.broadcasted_iota(jnp.int32, (tq, tq), 1)

    def causal(c):
        return col <= row + (i - c) * tq

    qi_all = qi_ref[0]
    wi = wi_ref[0]
    for h in range(IDX_HEADS):
        qim_scr[h] = jnp.where((lane >> 5) == h, qi_all, jnp.zeros_like(qi_all))
        wb_scr[h] = jnp.broadcast_to(wi[:, h:h + 1], (tq, LANES))
    q_all = q_ref[0]
    for h in range(DSA_HEADS):
        qm_scr[h] = jnp.where((lane >> 6) == h, q_all, jnp.zeros_like(q_all))

    def score_chunk(c, carry):
        start = pl.multiple_of(c * tq, tq)
        ki_c = ki_ref[0, pl.ds(start, tq), :]
        sc = jnp.zeros((tq, tq), F32)
        for h in range(IDX_HEADS):
            wb = wb_scr[h]
            sc = sc + jnp.maximum(_dot_nt(qim_scr[h], ki_c), 0.0) * jnp.concatenate([wb] * (tq // LANES), axis=1)
        sc = jnp.where(causal(c), sc, NEG_INF) + 0.0
        bits = lax.bitcast_convert_type(sc, jnp.int32)
        key_scr[c] = bits ^ ((bits >> 31) & jnp.int32(0x7FFFFFFF))
        return carry

    lax.fori_loop(0, n, score_chunk, 0)

    kf = jnp.float32(top_k)

    def bisect_over(n_chunks):
        def count_ge(cand):
            acc = jnp.zeros((tq, LANES), F32)
            for c in range(n_chunks):
                acc = acc + _fold_lanes(jnp.where(key_scr[c] >= cand, 1.0, 0.0))
            return jnp.sum(acc, axis=-1, keepdims=True)

        t0 = jnp.where(count_ge(jnp.zeros((tq, 1), jnp.int32)) >= kf, jnp.int32(0), jnp.int32(INT_MIN))

        def bisect(b, t):
            cand = t | lax.shift_left(jnp.int32(1), 30 - b)
            return jnp.where(count_ge(cand) >= kf, cand, t)

        thr_scr[...] = lax.fori_loop(0, 31, bisect, t0)

    for n_chunks in range(1, key_scr.shape[0] + 1):
        pl.when(n == n_chunks)(functools.partial(bisect_over, n_chunks))
    thr = thr_scr[...]

    def stats_chunk(c, carry):
        n_gt, n_eq, n_eqv = carry
        key = key_scr[c]
        eq = key == thr
        sel_scr[c] = jnp.where(key >= thr, 1.0, 0.0)
        return (n_gt + _fold_lanes(jnp.where(key > thr, 1.0, 0.0)),
                n_eq + _fold_lanes(jnp.where(eq, 1.0, 0.0)),
                n_eqv + _fold_lanes(jnp.where(eq & causal(c), 1.0, 0.0)))

    zeros = jnp.zeros((tq, LANES), F32)
    n_gt, n_eq, n_eqv = [jnp.sum(a, axis=-1, keepdims=True)
                         for a in lax.fori_loop(0, n, stats_chunk, (zeros, zeros, zeros))]
    room = kf - n_gt
    need_rank = jnp.max(jnp.where((n_eq > room) & (n_eqv > 0.0), 1.0, 0.0)) > 0.0

    @pl.when(need_rank)
    def _():
        upper = jnp.where(row < col, 1.0, 0.0).astype(BF16)

        def rank_chunk(c, before):
            key_c = key_scr[c]
            eq_c = jnp.where(key_c == thr, 1.0, 0.0)
            rank = _dot(eq_c.astype(BF16), upper) + before
            keep = (key_c > thr) | ((key_c == thr) & (rank < room))
            sel_scr[c] = jnp.where(keep, 1.0, 0.0)
            return before + jnp.sum(eq_c, axis=-1, keepdims=True)

        lax.fori_loop(0, n, rank_chunk, jnp.zeros((tq, 1), F32))

    def attn_chunk(c, states):
        start = pl.multiple_of(c * tq, tq)
        k_c = k_ref[0, pl.ds(start, tq), :]
        v_c = v_ref[0, pl.ds(start, tq), :]
        ok = (sel_scr[c] > 0.0) & causal(c)
        return tuple(_softmax_update(jnp.where(ok, _dot_nt(qm_scr[h], k_c), NEG_INF), v_c, states[h])
                     for h in range(DSA_HEADS))

    init = tuple((jnp.full((tq, 1), NEG_INF, F32), jnp.zeros((tq, 1), F32), jnp.zeros((tq, 256), F32))
                 for _ in range(DSA_HEADS))
    states = lax.fori_loop(0, n, attn_chunk, init)
    out = jnp.zeros((tq, 256), F32)
    for h in range(DSA_HEADS):
        _, l_h, acc_h = states[h]
        out = jnp.where((lane >> 6) == h, acc_h / l_h, out)
    o_ref[0] = out.astype(BF16)


def _dsa_attention(dq, dk, dv, qi, ki, wi, b, s, tq):
    top_k = min(DSA_TOPK_MAX, s // 4)
    assert tq >= top_k and tq % LANES == 0
    qspec = lambda n: pl.BlockSpec((1, tq, n), lambda bi, i: (bi, i, 0))
    kspec = lambda n: pl.BlockSpec((1, s, n), lambda bi, i: (bi, 0, 0))
    nc = s // tq
    scratch = [pltpu.VMEM((nc, tq, tq), jnp.int32), pltpu.VMEM((nc, tq, tq), F32),
               pltpu.VMEM((IDX_HEADS, tq, 256), BF16), pltpu.VMEM((IDX_HEADS, tq, LANES), F32),
               pltpu.VMEM((DSA_HEADS, tq, 256), BF16), pltpu.VMEM((tq, 1), jnp.int32)]
    est = (3 * tq * 256 * 2 * 2 + 3 * s * 256 * 2 * 2 + tq * LANES * 4 * 2 + 2 * tq * s * 4
           + 12 * tq * 256 * 2 + 8 * tq * LANES * 4 + 6 * tq * 256 * 4 + 16 * tq * tq * 4)
    r3 = lambda a: a.reshape(b, s, -1)
    return pl.pallas_call(
        functools.partial(_dsa_kernel, tq=tq, top_k=top_k),
        grid=(b, nc),
        in_specs=[qspec(256), kspec(256), kspec(256), qspec(256), kspec(256), qspec(LANES)],
        out_specs=qspec(256),
        out_shape=jax.ShapeDtypeStruct((b, s, 256), BF16),
        scratch_shapes=scratch,
        compiler_params=pltpu.CompilerParams(dimension_semantics=("arbitrary",) * 2,
                                             vmem_limit_bytes=_vmem_limit(est)),
        name="dsa_attn",
    )(r3(dq), r3(dk), r3(dv), r3(qi), r3(ki), r3(wi))


def _dilated_kernel(q_ref, kp_ref, kc_ref, vp_ref, vc_ref, o_ref, lse_ref, *, w, nrb, nres):
    j = pl.program_id(2)
    lane = lax.broadcasted_iota(jnp.int32, (w, 256), 1)
    row = lax.broadcasted_iota(jnp.int32, (w, 2 * w), 0)
    col = lax.broadcasted_iota(jnp.int32, (w, 2 * w), 1)
    rel = row + w - col
    in_window = (rel >= 0) & (rel <= w)
    first_rows_ok = in_window & (col + j * (2 * w) >= w)
    for rs in range(nres):
        cs = slice(rs * 256, (rs + 1) * 256)
        for rb in range(nrb):
            rows = slice(rb * w, (rb + 1) * w)
            q_all = q_ref[0, rows, cs]
            if rb == 0:
                k_prev, v_prev, ok = kp_ref[0, :, cs], vp_ref[0, :, cs], first_rows_ok
            else:
                prev_rows = slice((rb - 1) * w, rb * w)
                k_prev, v_prev, ok = kc_ref[0, prev_rows, cs], vc_ref[0, prev_rows, cs], in_window
            k = jnp.concatenate([k_prev, kc_ref[0, rows, cs]], axis=0)
            v = jnp.concatenate([v_prev, vc_ref[0, rows, cs]], axis=0)
            out = jnp.zeros((w, 256), F32)
            lse = jnp.zeros((w, 256), F32)
            for h in range(DIL_HEADS):
                head = (lane >> 6) == h
                qm = jnp.where(head, q_all, jnp.zeros_like(q_all))
                sc = jnp.where(ok, _dot_nt(qm, k), NEG_INF)
                m = jnp.max(sc, axis=-1, keepdims=True)
                p = jnp.exp(sc - m)
                l = jnp.sum(p, axis=-1, keepdims=True)
                o = _dot(p.astype(BF16), v)
                out = jnp.where(head, o / l, out)
                lse = jnp.where(head, m + jnp.log(l), lse)
            o_ref[0, rows, cs] = out
            lse_ref[0, rows, cs] = lse


def _dilated_attention(q, k, v, g, b, s):
    window, d = DIL_CONFIGS[g]
    w = window // d
    steps = s // d
    assert steps % w == 0
    nrb = min(4, steps // w)
    nres = min(d, max(1, 4 // nrb))
    q, k, v = [a.reshape(b, steps, d * 256) for a in (q, k, v)]
    cur = pl.BlockSpec((1, nrb * w, nres * 256), lambda bi, c, j: (bi, j, c))
    prev = pl.BlockSpec((1, w, nres * 256), lambda bi, c, j: (bi, jnp.maximum(j * nrb - 1, 0), c))
    est = (3 * nrb * w * nres * 256 * 2 * 2 + 2 * w * nres * 256 * 2 * 2 + 2 * nrb * w * nres * 256 * 4 * 2
           + 16 * 8 * w * 2 * w * 4)
    return pl.pallas_call(
        functools.partial(_dilated_kernel, w=w, nrb=nrb, nres=nres),
        grid=(b, d // nres, steps // (nrb * w)),
        in_specs=[cur, prev, cur, prev, cur],
        out_specs=[cur, cur],
        out_shape=[jax.ShapeDtypeStruct((b, steps, d * 256), F32)] * 2,
        compiler_params=pltpu.CompilerParams(dimension_semantics=("arbitrary",) * 3,
                                             vmem_limit_bytes=_vmem_limit(est)),
        name=f"dilated_attn_{g}",
    )(q, k, k, v, v)


def _layer_norm(y, g, b):
    mu = jnp.mean(y, axis=-1, keepdims=True)
    yc = y - mu
    var = jnp.mean(yc * yc, axis=-1, keepdims=True)
    return yc * lax.rsqrt(var + LN_EPS) * g + b


def _outproj_kernel(x_ref, oa_ref, ob_ref, o0_ref, l0_ref, o1_ref, l1_ref, o2_ref, l2_ref,
                    wo_ref, g_ref, b_ref, rwh_ref, rwl_ref, rb_ref, x1_ref, gate_ref, tok_scr):
    tm = x_ref.shape[0]

    def token_major(ref, slot, d):
        for j in range(2):
            for r in range(d):
                c0 = r * 256 + j * LANES
                tok_scr[2 * slot + j, pl.ds(r, tm // d, stride=d), :] = ref[0, :, c0:c0 + LANES]
        return jnp.concatenate([tok_scr[2 * slot], tok_scr[2 * slot + 1]], axis=1)

    d1, d2 = DIL_CONFIGS[1][1], DIL_CONFIGS[2][1]
    o0, l0 = o0_ref[...], l0_ref[...]
    o1, l1 = token_major(o1_ref, 0, d1), token_major(l1_ref, 1, d1)
    o2, l2 = token_major(o2_ref, 2, d2), token_major(l2_ref, 3, d2)
    mx = jnp.maximum(jnp.maximum(l0, l1), l2)
    e0, e1, e2 = jnp.exp(l0 - mx), jnp.exp(l1 - mx), jnp.exp(l2 - mx)
    oc = (e0 * o0 + e1 * o1 + e2 * o2) / (e0 + e1 + e2)
    na = MLA_HEADS * MLA_V
    mix = (_dot(oa_ref[...], wo_ref[0, :na, :]) + _dot(ob_ref[...], wo_ref[0, na:na + 256, :])
           + _dot(oc.astype(BF16), wo_ref[0, na + 256:, :]))
    x1 = _layer_norm(DEEPNORM_ALPHA * x_ref[...] + mix, g_ref[0], b_ref[0])
    x1_ref[...] = x1

    xh = x1.astype(BF16)
    xl = (x1 - xh.astype(F32)).astype(BF16)
    logits = _dot(xh, rwh_ref[0]) + _dot(xl, rwh_ref[0]) + _dot(xh, rwl_ref[0]) + rb_ref[0]
    lane = lax.broadcasted_iota(jnp.int32, (tm, LANES), 1)
    big = jnp.int32(LANES)
    is_group = (lane >= N_EXPERTS) & (lane < N_EXPERTS + N_GROUPS)
    gl = jnp.where(is_group, logits, -jnp.inf)
    gmax = jnp.max(gl, axis=-1, keepdims=True)
    g_star = jnp.min(jnp.where(gl == gmax, lane - N_EXPERTS, big), axis=-1, keepdims=True)
    p_top = 1.0 / jnp.sum(jnp.exp(gl - gmax), axis=-1, keepdims=True)
    in_group = (lane < N_EXPERTS) & ((lane >> 3) == g_star)
    el = jnp.where(in_group, logits, -jnp.inf)
    v1 = jnp.max(el, axis=-1, keepdims=True)
    i1 = jnp.min(jnp.where(el == v1, lane, big), axis=-1, keepdims=True)
    el2 = jnp.where(lane == i1, -jnp.inf, el)
    v2 = jnp.max(el2, axis=-1, keepdims=True)
    i2 = jnp.min(jnp.where(el2 == v2, lane, big), axis=-1, keepdims=True)
    e21 = jnp.exp(v2 - v1)
    w1 = 1.0 / (1.0 + e21)
    w2 = e21 / (1.0 + e21)
    gate_ref[...] = p_top * (jnp.where(lane == i1, w1, 0.0) + jnp.where(lane == i2, w2, 0.0))


def _outproj(layer, x2d, oa, ob, dil, w, tm, b, s):
    t = x2d.shape[0]
    row = lambda n: pl.BlockSpec((tm, n), lambda i: (i, 0))
    wspec = lambda a: pl.BlockSpec((1,) + a.shape[1:], lambda i: (layer,) + (0,) * (a.ndim - 1))
    tiles_per_seq = s // tm
    res_major = lambda d: pl.BlockSpec((1, tm // d, d * 256), lambda i: (i // tiles_per_seq, i % tiles_per_seq, 0))
    d1, d2 = DIL_CONFIGS[1][1], DIL_CONFIGS[2][1]
    weights = [w["wo"], w["ln1_g"], w["ln1_b"], w["rwh"], w["rwl"], w["rb"]]
    est = (D_MODEL * D_MODEL * 2 * 2 + 2 * D_MODEL * LANES * 2 * 2 + tm * D_MODEL * 4 * 4
           + tm * (512 + 256) * 2 * 2 + 6 * tm * 256 * 4 * 2 + 4 * tm * 256 * 4 + tm * LANES * 4 * 2
           + 6 * tm * D_MODEL * 4)
    return pl.pallas_call(
        _outproj_kernel,
        grid=(t // tm,),
        in_specs=[row(D_MODEL), row(MLA_HEADS * MLA_V), row(256), row(256), row(256),
                  res_major(d1), res_major(d1), res_major(d2), res_major(d2)] + [wspec(a) for a in weights],
        out_specs=[row(D_MODEL), row(LANES)],
        out_shape=[jax.ShapeDtypeStruct((t, D_MODEL), F32), jax.ShapeDtypeStruct((t, LANES), F32)],
        scratch_shapes=[pltpu.VMEM((8, tm, LANES), F32)],
        compiler_params=pltpu.CompilerParams(dimension_semantics=("arbitrary",),
                                             vmem_limit_bytes=_vmem_limit(est)),
        name="outproj_ln_router",
    )(x2d, oa, ob, dil[0][0].reshape(t, 256), dil[0][1].reshape(t, 256),
      dil[1][0], dil[1][1], dil[2][0], dil[2][1], *weights)


def _moe_kernel(x_ref, gate_ref, wg_ref, wu_ref, wd_ref, g_ref, b_ref, o_ref, xb_scr, acc_scr):
    e = pl.program_id(1)
    tm = x_ref.shape[0]

    @pl.when(e == 0)
    def _():
        xb_scr[...] = x_ref[...].astype(BF16)
        acc_scr[...] = jnp.zeros_like(acc_scr)

    xb = xb_scr[...]
    a = _dot(xb, wg_ref[0, 0])
    u = _dot(xb, wu_ref[0, 0])
    lane = lax.broadcasted_iota(jnp.int32, (tm, LANES), 1)
    ge = jnp.sum(jnp.where(lane == e, gate_ref[...], 0.0), axis=-1, keepdims=True)
    h = (a / (1.0 + jnp.exp(-a))) * u * ge
    acc_scr[...] += _dot(h.astype(BF16), wd_ref[0, 0])

    @pl.when(e == pl.num_programs(1) - 1)
    def _():
        o_ref[...] = _layer_norm(DEEPNORM_ALPHA * x_ref[...] + acc_scr[...], g_ref[0], b_ref[0])


def _moe(layer, x1, gate, w, tm):
    t = x1.shape[0]
    f = EXPERT_HIDDEN
    est = (tm * D_MODEL * 4 * 4 + tm * LANES * 4 * 2 + 3 * D_MODEL * f * 2 * 2
           + tm * D_MODEL * 6 + 4 * tm * f * 4)
    return pl.pallas_call(
        _moe_kernel,
        grid=(t // tm, N_EXPERTS),
        in_specs=[pl.BlockSpec((tm, D_MODEL), lambda i, e: (i, 0)),
                  pl.BlockSpec((tm, LANES), lambda i, e: (i, 0)),
                  pl.BlockSpec((1, 1, D_MODEL, f), lambda i, e: (layer, e, 0, 0)),
                  pl.BlockSpec((1, 1, D_MODEL, f), lambda i, e: (layer, e, 0, 0)),
                  pl.BlockSpec((1, 1, f, D_MODEL), lambda i, e: (layer, e, 0, 0)),
                  pl.BlockSpec((1, 1, D_MODEL), lambda i, e: (layer, 0, 0)),
                  pl.BlockSpec((1, 1, D_MODEL), lambda i, e: (layer, 0, 0))],
        out_specs=pl.BlockSpec((tm, D_MODEL), lambda i, e: (i, 0)),
        out_shape=jax.ShapeDtypeStruct((t, D_MODEL), F32),
        scratch_shapes=[pltpu.VMEM((tm, D_MODEL), BF16), pltpu.VMEM((tm, D_MODEL), F32)],
        compiler_params=pltpu.CompilerParams(dimension_semantics=("arbitrary", "arbitrary"),
                                             vmem_limit_bytes=_vmem_limit(est)),
        name="moe_ln",
    )(x1, gate, w["wg"], w["wu"], w["wd"], w["ln2_g"], w["ln2_b"])


def _pick_tile(n, pref):
    t = min(pref, n)
    while n % t:
        t //= 2
    return t


def _prepare_weights(w_in, mla_q_norm, mla_kv_norm, mla_w_uq, mla_w_ukv, w_o, ln1_g, ln1_b,
                     router_group_w, router_group_b, router_expert_w, router_expert_b,
                     expert_w_gate, expert_w_up, expert_w_down, ln2_g, ln2_b):
    depth = w_in.shape[0]
    w_perm = _gather_cols(w_in, _inproj_column_map()).astype(BF16)
    vec = lambda a: a.reshape(depth, 1, -1)
    rw = jnp.concatenate([jnp.moveaxis(router_expert_w, 1, 2).reshape(depth, D_MODEL, N_EXPERTS),
                          router_group_w,
                          jnp.zeros((depth, D_MODEL, LANES - N_EXPERTS - N_GROUPS), F32)], axis=-1)
    rb = jnp.concatenate([router_expert_b.reshape(depth, N_EXPERTS), router_group_b,
                          jnp.zeros((depth, LANES - N_EXPERTS - N_GROUPS), F32)], axis=-1)
    rwh = rw.astype(BF16)
    ex = lambda a: a.reshape((depth, N_EXPERTS) + a.shape[3:]).astype(BF16)
    return {
        "wa": w_perm[:, :, :_WA], "wb": w_perm[:, :, _WA:_WA + _WB], "wc": w_perm[:, :, _WA + _WB:],
        "wuq": _gather_cols(mla_w_uq, _uq_column_map()).astype(BF16),
        "wukv": _gather_cols(mla_w_ukv, _ukv_column_map()).astype(BF16),
        "qn": vec(mla_q_norm), "kvn": vec(mla_kv_norm),
        "wo": w_o.astype(BF16), "ln1_g": vec(ln1_g), "ln1_b": vec(ln1_b),
        "rwh": rwh, "rwl": (rw - rwh.astype(F32)).astype(BF16), "rb": vec(rb),
        "wg": ex(expert_w_gate), "wu": ex(expert_w_up), "wd": ex(expert_w_down),
        "ln2_g": vec(ln2_g), "ln2_b": vec(ln2_b),
    }


def kernel(x, positions, w_in, mla_q_norm, mla_kv_norm, mla_w_uq, mla_w_ukv, w_o, ln1_g, ln1_b,
           router_group_w, router_group_b, router_expert_w, router_expert_b,
           expert_w_gate, expert_w_up, expert_w_down, ln2_g, ln2_b):
    b, s, d_model = x.shape
    assert d_model == D_MODEL and w_in.shape[-1] == _IN_COLS
    t = b * s
    w = _prepare_weights(w_in, mla_q_norm, mla_kv_norm, mla_w_uq, mla_w_ukv, w_o, ln1_g, ln1_b,
                         router_group_w, router_group_b, router_expert_w, router_expert_b,
                         expert_w_gate, expert_w_up, expert_w_down, ln2_g, ln2_b)
    tabs = [a.reshape(t, LANES) for a in _rope_tables(positions)]
    tm_proj = _pick_tile(s, 512)
    tm_moe = _pick_tile(t, 1024)
    tq_mla = _pick_tile(s, 512)
    tq_dsa = _pick_tile(s, 256)
    n_cfg = len(DIL_CONFIGS)
    x2d = x.reshape(t, D_MODEL)
    for layer in range(w_in.shape[0]):
        (mq, mk, mv, dq, dk, dv, qi, ki, wi), dil_qkv = _inproj(layer, x2d, w, tabs, tm_proj, b, s)
        oa = _mla_attention(mq, mk, mv, b, s, tq_mla).reshape(t, -1)
        ob = _dsa_attention(dq, dk, dv, qi, ki, wi, b, s, tq_dsa).reshape(t, -1)
        dil = [_dilated_attention(dil_qkv[g], dil_qkv[n_cfg + g], dil_qkv[2 * n_cfg + g], g, b, s)
               for g in range(n_cfg)]
        x1, gate = _outproj(layer, x2d, oa, ob, dil, w, tm_proj, b, s)
        x2d = _moe(layer, x1, gate, w, tm_moe)
    return x2d.reshape(b, s, D_MODEL)
```

```python
import functools

import numpy as np
import jax
import jax.numpy as jnp
from jax import lax
from jax.experimental import pallas as pl
from jax.experimental.pallas import tpu as pltpu

D_MODEL = 1024
DEPTH = 4
HEAD_DIM = 64
MLA_HEADS = 8
MLA_Q_RANK = 256
MLA_KV_RANK = 128
MLA_NOPE = 64
MLA_ROPE = 32
MLA_V = 64
DSA_HEADS = 4
IDX_HEADS = 8
IDX_DIM = 32
DSA_TOPK_MAX = 256
DIL_CONFIGS = ((128, 1), (512, 4), (2048, 16))
DIL_HEADS = 4
ROPE_THETA = 500000.0
ROT_HEAD_DIMS = HEAD_DIM // 4
ROT_IDX_DIMS = IDX_DIM // 4
N_GROUPS = 4
EXPERTS_PER_GROUP = 8
N_EXPERTS = N_GROUPS * EXPERTS_PER_GROUP
EXPERT_HIDDEN = 256
DEEPNORM_ALPHA = (2.0 * DEPTH) ** 0.25
LN_EPS = 1e-5
RMS_EPS = 1e-6
NEG_INF = -1e30

LANES = 128
VMEM_CAP_BYTES = 60000 * 1024
INT_MIN = -(2 ** 31)

BF16 = jnp.bfloat16
F32 = jnp.float32

_OFF_CQ = 0
_OFF_CKV = _OFF_CQ + MLA_Q_RANK
_OFF_KPE = _OFF_CKV + MLA_KV_RANK
_OFF_BQ = _OFF_KPE + MLA_ROPE
_OFF_BK = _OFF_BQ + DSA_HEADS * HEAD_DIM
_OFF_BV = _OFF_BK + HEAD_DIM
_OFF_BQI = _OFF_BV + HEAD_DIM
_OFF_BKI = _OFF_BQI + IDX_HEADS * IDX_DIM
_OFF_BWI = _OFF_BKI + IDX_DIM
_OFF_C = _OFF_BWI + IDX_HEADS
_N_DIL = len(DIL_CONFIGS) * DIL_HEADS * HEAD_DIM
_IN_COLS = _OFF_C + 3 * _N_DIL

_WA = MLA_Q_RANK + MLA_KV_RANK + LANES
_WB = 5 * 256 + LANES
_WC = 3 * _N_DIL


def _vmem_limit(est_bytes):
    return int(min(max(2 * est_bytes, 32 * 1024 * 1024), VMEM_CAP_BYTES))


def _dot(a, b):
    return jnp.dot(a, b, preferred_element_type=F32)


def _dot_nt(a, b):
    return lax.dot_general(a, b, (((1,), (1,)), ((), ())), preferred_element_type=F32)


def _inproj_column_map():
    a = list(range(_OFF_CQ, _OFF_CQ + MLA_Q_RANK)) + list(range(_OFF_CKV, _OFF_CKV + MLA_KV_RANK))
    kpe = [-1] * LANES
    kpe[MLA_NOPE:MLA_NOPE + MLA_ROPE] = range(_OFF_KPE, _OFF_KPE + MLA_ROPE)
    a += kpe
    b = list(range(_OFF_BQ, _OFF_BQ + 256))
    b += list(range(_OFF_BK, _OFF_BK + HEAD_DIM)) * DSA_HEADS
    b += list(range(_OFF_BV, _OFF_BV + HEAD_DIM)) * DSA_HEADS
    b += list(range(_OFF_BQI, _OFF_BQI + 256))
    b += list(range(_OFF_BKI, _OFF_BKI + IDX_DIM)) * IDX_HEADS
    b += list(range(_OFF_BWI, _OFF_BWI + IDX_HEADS)) + [-1] * (LANES - IDX_HEADS)
    c = list(range(_OFF_C, _OFF_C + _WC))
    assert len(a) == _WA and len(b) == _WB and len(c) == _WC
    return np.asarray(a + b + c, np.int32)


def _gather_cols(w, cols):
    cols = np.asarray(cols, np.int32)
    g = jnp.take(w, jnp.asarray(np.maximum(cols, 0)), axis=-1)
    return jnp.where(jnp.asarray(cols >= 0), g, 0.0)


def _uq_column_map():
    cols = []
    for h in range(MLA_HEADS):
        base = h * (MLA_NOPE + MLA_ROPE)
        cols += list(range(base, base + MLA_NOPE + MLA_ROPE)) + [-1] * (LANES - MLA_NOPE - MLA_ROPE)
    return cols


def _ukv_column_map():
    kcols, vcols = [], []
    for h in range(MLA_HEADS):
        base = h * (MLA_NOPE + MLA_V)
        kcols += list(range(base, base + MLA_NOPE)) + [-1] * (LANES - MLA_NOPE)
        vcols += list(range(base + MLA_NOPE, base + MLA_NOPE + MLA_V))
    return kcols + vcols


def _rope_tables(positions):
    pos = positions.astype(F32)[..., None]

    def cs(rot_dims):
        inv = ROPE_THETA ** (-jnp.arange(0, rot_dims, 2, dtype=F32) / rot_dims)
        ang = pos * inv
        return jnp.cos(ang), jnp.sin(ang)

    def pattern(c, s, period, start):
        half = c.shape[-1]
        one = jnp.ones(c.shape[:-1] + (period,), F32)
        zero = jnp.zeros_like(one)
        cc = one.at[..., start:start + half].set(c).at[..., start + half:start + 2 * half].set(c)
        ss = zero.at[..., start:start + half].set(-s).at[..., start + half:start + 2 * half].set(s)
        reps = LANES // period
        return jnp.tile(cc, (1, 1, reps)), jnp.tile(ss, (1, 1, reps))

    cm, sm = pattern(*cs(MLA_ROPE), LANES, MLA_NOPE)
    ch, sh = pattern(*cs(ROT_HEAD_DIMS), HEAD_DIM, 0)
    ci, si = pattern(*cs(ROT_IDX_DIMS), IDX_DIM, 0)
    return cm, sm, ch, sh, ci, si


def _rope_block(u, c, s, first_half, half):
    rot = jnp.where(first_half, pltpu.roll(u, LANES - half, 1), pltpu.roll(u, half, 1))
    return u * c + rot * s


def _inproj_kernel(x_ref, wa_ref, wb_ref, wc_ref, wuq_ref, wukv_ref, qn_ref, kvn_ref,
                   cm_ref, sm_ref, ch_ref, sh_ref, ci_ref, si_ref,
                   mq_ref, mk_ref, mv_ref, dq_ref, dk_ref, dv_ref, qi_ref, ki_ref, wi_ref,
                   *dil_refs_and_scratch):
    dil_refs, perm_scr = dil_refs_and_scratch[:-1], dil_refs_and_scratch[-1]
    tm = x_ref.shape[0]
    lane = lax.broadcasted_iota(jnp.int32, (tm, LANES), 1)
    first_m = (lane >= MLA_NOPE) & (lane < MLA_NOPE + MLA_ROPE // 2)
    first_h = (lane & (HEAD_DIM - 1)) < ROT_HEAD_DIMS // 2
    first_i = (lane & (IDX_DIM - 1)) < ROT_IDX_DIMS // 2
    cm, sm = cm_ref[...], sm_ref[...]
    ch, sh = ch_ref[...], sh_ref[...]
    ci, si = ci_ref[...], si_ref[...]
    rope_m = functools.partial(_rope_block, c=cm, s=sm, first_half=first_m, half=MLA_ROPE // 2)
    rope_h = functools.partial(_rope_block, c=ch, s=sh, first_half=first_h, half=ROT_HEAD_DIMS // 2)
    rope_i = functools.partial(_rope_block, c=ci, s=si, first_half=first_i, half=ROT_IDX_DIMS // 2)

    xb = x_ref[...].astype(BF16)

    ua = _dot(xb, wa_ref[0])
    cq = ua[:, :MLA_Q_RANK]
    cqn = cq * lax.rsqrt(jnp.mean(cq * cq, axis=-1, keepdims=True) + RMS_EPS) * qn_ref[0]
    q = _dot(cqn.astype(BF16), wuq_ref[0])
    mla_scale = (MLA_NOPE + MLA_ROPE) ** -0.5
    for h in range(MLA_HEADS):
        sl = slice(h * LANES, (h + 1) * LANES)
        mq_ref[:, sl] = (rope_m(q[:, sl]) * mla_scale).astype(BF16)
    ckv = ua[:, MLA_Q_RANK:MLA_Q_RANK + MLA_KV_RANK]
    ckvn = ckv * lax.rsqrt(jnp.mean(ckv * ckv, axis=-1, keepdims=True) + RMS_EPS) * kvn_ref[0]
    kv = _dot(ckvn.astype(BF16), wukv_ref[0])
    kpe = rope_m(ua[:, MLA_Q_RANK + MLA_KV_RANK:])
    for h in range(MLA_HEADS):
        sl = slice(h * LANES, (h + 1) * LANES)
        mk_ref[:, sl] = (kv[:, sl] + kpe).astype(BF16)
    mv_ref[...] = kv[:, MLA_HEADS * LANES:].astype(BF16)

    ub = _dot(xb, wb_ref[0])
    dsa_scale = HEAD_DIM ** -0.5
    idx_scale = IDX_DIM ** -0.5 * IDX_HEADS ** -0.5
    for j in range(2):
        sl = slice(j * LANES, (j + 1) * LANES)
        dq_ref[:, sl] = (rope_h(ub[:, sl]) * dsa_scale).astype(BF16)
        dk_ref[:, sl] = rope_h(ub[:, 256 + j * LANES:256 + (j + 1) * LANES]).astype(BF16)
        dv_ref[:, sl] = ub[:, 512 + j * LANES:512 + (j + 1) * LANES].astype(BF16)
        qi_ref[:, sl] = rope_i(ub[:, 768 + j * LANES:768 + (j + 1) * LANES]).astype(BF16)
        ki_ref[:, sl] = rope_i(ub[:, 1024 + j * LANES:1024 + (j + 1) * LANES]).astype(BF16)
    wi_ref[...] = ub[:, 1280:] * idx_scale

    uc = _dot(xb, wc_ref[0])
    n_cfg = len(DIL_CONFIGS)
    for which in range(3):
        for g, (_, d) in enumerate(DIL_CONFIGS):
            halves = []
            for j in range(2):
                c0 = which * _N_DIL + g * 256 + j * LANES
                blk = uc[:, c0:c0 + LANES]
                if which < 2:
                    blk = rope_h(blk)
                if which == 0:
                    blk = blk * dsa_scale
                halves.append(blk)
            out_ref = dil_refs[which * n_cfg + g]
            if d == 1:
                out_ref[:, :LANES] = halves[0].astype(BF16)
                out_ref[:, LANES:] = halves[1].astype(BF16)
            else:
                for j in range(2):
                    slot = 2 * (which * (n_cfg - 1) + g - 1) + j
                    perm_scr[slot] = halves[j]
                    for r in range(d):
                        rows = perm_scr[slot, pl.ds(r, tm // d, stride=d), :]
                        out_ref[0, :, r * 256 + j * LANES:r * 256 + (j + 1) * LANES] = rows.astype(BF16)


def _inproj(layer, x2d, w, tabs, tm, b, s):
    t = x2d.shape[0]
    row = lambda n: pl.BlockSpec((tm, n), lambda i: (i, 0))
    wspec = lambda a: pl.BlockSpec((1,) + a.shape[1:], lambda i: (layer,) + (0,) * (a.ndim - 1))
    outs = [(8 * LANES, BF16), (8 * LANES, BF16), (MLA_HEADS * MLA_V, BF16),
            (256, BF16), (256, BF16), (256, BF16), (256, BF16), (256, BF16), (LANES, F32)]
    weights = [w["wa"], w["wb"], w["wc"], w["wuq"], w["wukv"], w["qn"], w["kvn"]]
    tiles_per_seq = s // tm
    dil_specs, dil_shapes = [], []
    for _ in range(3):
        for _, d in DIL_CONFIGS:
            if d == 1:
                dil_specs.append(row(256))
                dil_shapes.append(jax.ShapeDtypeStruct((t, 256), BF16))
            else:
                dil_specs.append(pl.BlockSpec((1, tm // d, d * 256),
                                              lambda i: (i // tiles_per_seq, i % tiles_per_seq, 0)))
                dil_shapes.append(jax.ShapeDtypeStruct((b, s // d, d * 256), BF16))
    est = (sum(int(np.prod(a.shape[1:])) * a.dtype.itemsize for a in weights) * 2
           + tm * D_MODEL * 4 * 2 + 6 * tm * LANES * 4 * 2
           + (sum(tm * n * jnp.dtype(d).itemsize for n, d in outs) + 9 * tm * 256 * 2) * 2
           + 6 * tm * 256 * 4 + tm * (_WA + _WB + _WC + 8 * LANES + 12 * LANES) * 4)
    res = pl.pallas_call(
        _inproj_kernel,
        grid=(t // tm,),
        in_specs=[row(D_MODEL)] + [wspec(a) for a in weights] + [row(LANES)] * 6,
        out_specs=[row(n) for n, _ in outs] + dil_specs,
        out_shape=[jax.ShapeDtypeStruct((t, n), d) for n, d in outs] + dil_shapes,
        scratch_shapes=[pltpu.VMEM((2 * 3 * (len(DIL_CONFIGS) - 1), tm, LANES), F32)],
        compiler_params=pltpu.CompilerParams(dimension_semantics=("arbitrary",),
                                             vmem_limit_bytes=_vmem_limit(est)),
        name="inproj",
    )(x2d, *weights, *tabs)
    return res[:len(outs)], res[len(outs):]


def _softmax_updates(scores, values, states):
    ps, partial = [], []
    for s, (m_prev, l_prev, acc_prev) in zip(scores, states):
        m_new = jnp.maximum(m_prev, jnp.max(s, axis=-1, keepdims=True))
        alpha = jnp.exp(m_prev - m_new)
        p = jnp.exp(s - m_new)
        ps.append(p.astype(BF16))
        partial.append((m_new, alpha * l_prev + jnp.sum(p, axis=-1, keepdims=True), alpha * acc_prev))
    return tuple((m, l, acc + _dot(p, v)) for (m, l, acc), p, v in zip(partial, ps, values))


def _mla_kernel(q_ref, k_ref, v_ref, o_ref, *, tq, nh):
    qi = pl.program_id(2)
    lane = lax.broadcasted_iota(jnp.int32, (tq, LANES), 1)
    row = lax.broadcasted_iota(jnp.int32, (tq, tq), 0)
    col = lax.broadcasted_iota(jnp.int32, (tq, tq), 1)

    def step(j, states, diagonal):
        start = pl.multiple_of(j * tq, tq)
        scores, values = [], []
        for h in range(nh):
            hs = slice(h * LANES, (h + 1) * LANES)
            vs = slice((h // 2) * LANES, (h // 2 + 1) * LANES)
            s = _dot_nt(q_ref[0, :, hs], k_ref[0, pl.ds(start, tq), hs])
            scores.append(jnp.where(col <= row, s, NEG_INF) if diagonal else s)
            values.append(v_ref[0, pl.ds(start, tq), vs])
        return _softmax_updates(scores, values, states)

    init = tuple((jnp.full((tq, 1), NEG_INF, F32), jnp.zeros((tq, 1), F32), jnp.zeros((tq, LANES), F32))
                 for _ in range(nh))
    states = lax.fori_loop(0, qi, lambda j, st: step(j, st, False), init)
    states = step(qi, states, True)
    for pair in range(nh // 2):
        (_, l_even, acc_even), (_, l_odd, acc_odd) = states[2 * pair], states[2 * pair + 1]
        o_ref[0, :, pair * LANES:(pair + 1) * LANES] = jnp.where(
            lane < MLA_V, acc_even / l_even, acc_odd / l_odd).astype(BF16)


def _mla_attention(mq, mk, mv, b, s, tq, nh=4):
    est = (tq * nh * LANES * 2 * 2 + s * nh * LANES * 2 * 2 + s * nh * MLA_V * 2 * 2 + tq * nh * MLA_V * 2 * 2
           + nh * tq * 3 * LANES * 4 + 4 * nh * tq * tq * 4)
    return pl.pallas_call(
        functools.partial(_mla_kernel, tq=tq, nh=nh),
        grid=(b, MLA_HEADS // nh, s // tq),
        in_specs=[pl.BlockSpec((1, tq, nh * LANES), lambda bi, h, i: (bi, i, h)),
                  pl.BlockSpec((1, s, nh * LANES), lambda bi, h, i: (bi, 0, h)),
                  pl.BlockSpec((1, s, nh * MLA_V), lambda bi, h, i: (bi, 0, h))],
        out_specs=pl.BlockSpec((1, tq, nh * MLA_V), lambda bi, h, i: (bi, i, h)),
        out_shape=jax.ShapeDtypeStruct((b, s, MLA_HEADS * MLA_V), BF16),
        compiler_params=pltpu.CompilerParams(dimension_semantics=("arbitrary",) * 3,
                                             vmem_limit_bytes=_vmem_limit(est)),
        name="mla_attn",
    )(mq.reshape(b, s, -1), mk.reshape(b, s, -1), mv.reshape(b, s, -1))


def _fold_lanes(x):
    acc = x[:, :LANES]
    for part in range(1, x.shape[1] // LANES):
        acc = acc + x[:, part * LANES:(part + 1) * LANES]
    return acc


def _dsa_kernel(q_ref, k_ref, v_ref, qi_ref, ki_ref, wi_ref, o_ref,
                key_scr, sel_scr, qim_scr, wb_scr, qm_scr, thr_scr, *, tq, top_k):
    i = pl.program_id(1)
    n = i + 1
    lane = lax.broadcasted_iota(jnp.int32, (tq, 256), 1)
    row = lax.broadcasted_iota(jnp.int32, (tq, tq), 0)
    col = lax.broadcasted_iota(jnp.int32, (tq, tq), 1)

    def causal(c):
        return col <= row + (i - c) * tq

    qi_all = qi_ref[0]
    wi = wi_ref[0]
    for h in range(IDX_HEADS):
        qim_scr[h] = jnp.where((lane >> 5) == h, qi_all, jnp.zeros_like(qi_all))
        wb_scr[h] = jnp.broadcast_to(wi[:, h:h + 1], (tq, LANES))
    q_all = q_ref[0]
    for h in range(DSA_HEADS):
        qm_scr[h] = jnp.where((lane >> 6) == h, q_all, jnp.zeros_like(q_all))

    def score_chunk(c, carry):
        start = pl.multiple_of(c * tq, tq)
        ki_c = ki_ref[0, pl.ds(start, tq), :]
        sc = jnp.zeros((tq, tq), F32)
        for h in range(IDX_HEADS):
            wb = wb_scr[h]
            sc = sc + jnp.maximum(_dot_nt(qim_scr[h], ki_c), 0.0) * jnp.concatenate([wb] * (tq // LANES), axis=1)
        sc = jnp.where(causal(c), sc, NEG_INF) + 0.0
        bits = lax.bitcast_convert_type(sc, jnp.int32)
        key_scr[c] = bits ^ ((bits >> 31) & jnp.int32(0x7FFFFFFF))
        return carry

    lax.fori_loop(0, n, score_chunk, 0)

    kf = jnp.float32(top_k)

    def bisect_over(n_chunks):
        def count_ge(cand):
            acc = jnp.zeros((tq, LANES), F32)
            for c in range(n_chunks):
                acc = acc + _fold_lanes(jnp.where(key_scr[c] >= cand, 1.0, 0.0))
            return jnp.sum(acc, axis=-1, keepdims=True)

        t0 = jnp.where(count_ge(jnp.zeros((tq, 1), jnp.int32)) >= kf, jnp.int32(0), jnp.int32(INT_MIN))

        def bisect(b, t):
            cand = t | lax.shift_left(jnp.int32(1), 30 - b)
            return jnp.where(count_ge(cand) >= kf, cand, t)

        thr_scr[...] = lax.fori_loop(0, 31, bisect, t0)

    for n_chunks in range(1, key_scr.shape[0] + 1):
        pl.when(n == n_chunks)(functools.partial(bisect_over, n_chunks))
    thr = thr_scr[...]

    def stats_chunk(c, carry):
        n_gt, n_eq, n_eqv = carry
        key = key_scr[c]
        eq = key == thr
        sel_scr[c] = jnp.where(key >= thr, 1.0, 0.0)
        return (n_gt + _fold_lanes(jnp.where(key > thr, 1.0, 0.0)),
                n_eq + _fold_lanes(jnp.where(eq, 1.0, 0.0)),
                n_eqv + _fold_lanes(jnp.where(eq & causal(c), 1.0, 0.0)))

    zeros = jnp.zeros((tq, LANES), F32)
    n_gt, n_eq, n_eqv = [jnp.sum(a, axis=-1, keepdims=True)
                         for a in lax.fori_loop(0, n, stats_chunk, (zeros, zeros, zeros))]
    room = kf - n_gt
    need_rank = jnp.max(jnp.where((n_eq > room) & (n_eqv > 0.0), 1.0, 0.0)) > 0.0

    @pl.when(need_rank)
    def _():
        upper = jnp.where(row < col, 1.0, 0.0).astype(BF16)

        def rank_chunk(c, before):
            key_c = key_scr[c]
            eq_c = jnp.where(key_c == thr, 1.0, 0.0)
            rank = _dot(eq_c.astype(BF16), upper) + before
            keep = (key_c > thr) | ((key_c == thr) & (rank < room))
            sel_scr[c] = jnp.where(keep, 1.0, 0.0)
            return before + jnp.sum(eq_c, axis=-1, keepdims=True)

        lax.fori_loop(0, n, rank_chunk, jnp.zeros((tq, 1), F32))

    def attn_chunk(c, states):
        start = pl.multiple_of(c * tq, tq)
        k_c = k_ref[0, pl.ds(start, tq), :]
        v_c = v_ref[0, pl.ds(start, tq), :]
        ok = (sel_scr[c] > 0.0) & causal(c)
        scores = [jnp.where(ok, _dot_nt(qm_scr[h], k_c), NEG_INF) for h in range(DSA_HEADS)]
        return _softmax_updates(scores, [v_c] * DSA_HEADS, states)

    init = tuple((jnp.full((tq, 1), NEG_INF, F32), jnp.zeros((tq, 1), F32), jnp.zeros((tq, 256), F32))
                 for _ in range(DSA_HEADS))
    states = lax.fori_loop(0, n, attn_chunk, init)
    out = jnp.zeros((tq, 256), F32)
    for h in range(DSA_HEADS):
        _, l_h, acc_h = states[h]
        out = jnp.where((lane >> 6) == h, acc_h / l_h, out)
    o_ref[0] = out.astype(BF16)


def _dsa_attention(dq, dk, dv, qi, ki, wi, b, s, tq):
    top_k = min(DSA_TOPK_MAX, s // 4)
    assert tq >= top_k and tq % LANES == 0
    qspec = lambda n: pl.BlockSpec((1, tq, n), lambda bi, i: (bi, i, 0))
    kspec = lambda n: pl.BlockSpec((1, s, n), lambda bi, i: (bi, 0, 0))
    nc = s // tq
    scratch = [pltpu.VMEM((nc, tq, tq), jnp.int32), pltpu.VMEM((nc, tq, tq), F32),
               pltpu.VMEM((IDX_HEADS, tq, 256), BF16), pltpu.VMEM((IDX_HEADS, tq, LANES), F32),
               pltpu.VMEM((DSA_HEADS, tq, 256), BF16), pltpu.VMEM((tq, 1), jnp.int32)]
    est = (3 * tq * 256 * 2 * 2 + 3 * s * 256 * 2 * 2 + tq * LANES * 4 * 2 + 2 * tq * s * 4
           + 12 * tq * 256 * 2 + 8 * tq * LANES * 4 + 6 * tq * 256 * 4 + 16 * tq * tq * 4)
    r3 = lambda a: a.reshape(b, s, -1)
    return pl.pallas_call(
        functools.partial(_dsa_kernel, tq=tq, top_k=top_k),
        grid=(b, nc),
        in_specs=[qspec(256), kspec(256), kspec(256), qspec(256), kspec(256), qspec(LANES)],
        out_specs=qspec(256),
        out_shape=jax.ShapeDtypeStruct((b, s, 256), BF16),
        scratch_shapes=scratch,
        compiler_params=pltpu.CompilerParams(dimension_semantics=("arbitrary",) * 2,
                                             vmem_limit_bytes=_vmem_limit(est)),
        name="dsa_attn",
    )(r3(dq), r3(dk), r3(dv), r3(qi), r3(ki), r3(wi))


def _dilated_kernel(q_ref, kp_ref, kc_ref, vp_ref, vc_ref, o_ref, lse_ref, *, w, nrb, nres):
    j = pl.program_id(2)
    lane = lax.broadcasted_iota(jnp.int32, (w, 256), 1)
    row = lax.broadcasted_iota(jnp.int32, (w, 2 * w), 0)
    col = lax.broadcasted_iota(jnp.int32, (w, 2 * w), 1)
    rel = row + w - col
    in_window = (rel >= 0) & (rel <= w)
    first_rows_ok = in_window & (col + j * (2 * w) >= w)
    heads = [(lane >> 6) == h for h in range(DIL_HEADS)]
    units = []
    for rs in range(nres):
        cs = slice(rs * 256, (rs + 1) * 256)
        for rb in range(nrb):
            rows = slice(rb * w, (rb + 1) * w)
            q_all = q_ref[0, rows, cs]
            if rb == 0:
                k_prev, v_prev, ok = kp_ref[0, :, cs], vp_ref[0, :, cs], first_rows_ok
            else:
                prev_rows = slice((rb - 1) * w, rb * w)
                k_prev, v_prev, ok = kc_ref[0, prev_rows, cs], vc_ref[0, prev_rows, cs], in_window
            k = jnp.concatenate([k_prev, kc_ref[0, rows, cs]], axis=0)
            v = jnp.concatenate([v_prev, vc_ref[0, rows, cs]], axis=0)
            scores = [jnp.where(ok, _dot_nt(jnp.where(head, q_all, jnp.zeros_like(q_all)), k), NEG_INF)
                      for head in heads]
            units.append((rows, cs, v, scores))
    soft = []
    for rows, cs, v, scores in units:
        stats = []
        for sc in scores:
            m = jnp.max(sc, axis=-1, keepdims=True)
            p = jnp.exp(sc - m)
            stats.append((m, jnp.sum(p, axis=-1, keepdims=True), p.astype(BF16)))
        soft.append((rows, cs, v, stats))
    for rows, cs, v, stats in soft:
        out = jnp.zeros((w, 256), F32)
        lse = jnp.zeros((w, 256), F32)
        for head, (m, l, p) in zip(heads, stats):
            out = jnp.where(head, _dot(p, v) / l, out)
            lse = jnp.where(head, m + jnp.log(l), lse)
        o_ref[0, rows, cs] = out
        lse_ref[0, rows, cs] = lse


def _dilated_attention(q, k, v, g, b, s):
    window, d = DIL_CONFIGS[g]
    w = window // d
    steps = s // d
    assert steps % w == 0
    nrb = min(4, steps // w)
    nres = min(d, max(1, 4 // nrb))
    q, k, v = [a.reshape(b, steps, d * 256) for a in (q, k, v)]
    cur = pl.BlockSpec((1, nrb * w, nres * 256), lambda bi, c, j: (bi, j, c))
    prev = pl.BlockSpec((1, w, nres * 256), lambda bi, c, j: (bi, jnp.maximum(j * nrb - 1, 0), c))
    est = (3 * nrb * w * nres * 256 * 2 * 2 + 2 * w * nres * 256 * 2 * 2 + 2 * nrb * w * nres * 256 * 4 * 2
           + 16 * 8 * w * 2 * w * 4)
    return pl.pallas_call(
        functools.partial(_dilated_kernel, w=w, nrb=nrb, nres=nres),
        grid=(b, d // nres, steps // (nrb * w)),
        in_specs=[cur, prev, cur, prev, cur],
        out_specs=[cur, cur],
        out_shape=[jax.ShapeDtypeStruct((b, steps, d * 256), F32)] * 2,
        compiler_params=pltpu.CompilerParams(dimension_semantics=("arbitrary",) * 3,
                                             vmem_limit_bytes=_vmem_limit(est)),
        name=f"dilated_attn_{g}",
    )(q, k, k, v, v)


def _layer_norm(y, g, b):
    mu = jnp.mean(y, axis=-1, keepdims=True)
    yc = y - mu
    var = jnp.mean(yc * yc, axis=-1, keepdims=True)
    return yc * lax.rsqrt(var + LN_EPS) * g + b


def _outproj_kernel(x_ref, oa_ref, ob_ref, o0_ref, l0_ref, o1_ref, l1_ref, o2_ref, l2_ref,
                    wo_ref, g_ref, b_ref, rwh_ref, rwl_ref, rb_ref, x1_ref, gate_ref, tok_scr):
    tm = x_ref.shape[0]

    def token_major(ref, slot, d):
        for j in range(2):
            for r in range(d):
                c0 = r * 256 + j * LANES
                tok_scr[2 * slot + j, pl.ds(r, tm // d, stride=d), :] = ref[0, :, c0:c0 + LANES]
        return jnp.concatenate([tok_scr[2 * slot], tok_scr[2 * slot + 1]], axis=1)

    d1, d2 = DIL_CONFIGS[1][1], DIL_CONFIGS[2][1]
    o0, l0 = o0_ref[...], l0_ref[...]
    o1, l1 = token_major(o1_ref, 0, d1), token_major(l1_ref, 1, d1)
    o2, l2 = token_major(o2_ref, 2, d2), token_major(l2_ref, 3, d2)
    mx = jnp.maximum(jnp.maximum(l0, l1), l2)
    e0, e1, e2 = jnp.exp(l0 - mx), jnp.exp(l1 - mx), jnp.exp(l2 - mx)
    oc = (e0 * o0 + e1 * o1 + e2 * o2) / (e0 + e1 + e2)
    na = MLA_HEADS * MLA_V
    mix = (_dot(oa_ref[...], wo_ref[0, :na, :]) + _dot(ob_ref[...], wo_ref[0, na:na + 256, :])
           + _dot(oc.astype(BF16), wo_ref[0, na + 256:, :]))
    x1 = _layer_norm(DEEPNORM_ALPHA * x_ref[...] + mix, g_ref[0], b_ref[0])
    x1_ref[...] = x1

    xh = x1.astype(BF16)
    xl = (x1 - xh.astype(F32)).astype(BF16)
    logits = _dot(xh, rwh_ref[0]) + _dot(xl, rwh_ref[0]) + _dot(xh, rwl_ref[0]) + rb_ref[0]
    lane = lax.broadcasted_iota(jnp.int32, (tm, LANES), 1)
    big = jnp.int32(LANES)
    is_group = (lane >= N_EXPERTS) & (lane < N_EXPERTS + N_GROUPS)
    gl = jnp.where(is_group, logits, -jnp.inf)
    gmax = jnp.max(gl, axis=-1, keepdims=True)
    g_star = jnp.min(jnp.where(gl == gmax, lane - N_EXPERTS, big), axis=-1, keepdims=True)
    p_top = 1.0 / jnp.sum(jnp.exp(gl - gmax), axis=-1, keepdims=True)
    in_group = (lane < N_EXPERTS) & ((lane >> 3) == g_star)
    el = jnp.where(in_group, logits, -jnp.inf)
    v1 = jnp.max(el, axis=-1, keepdims=True)
    i1 = jnp.min(jnp.where(el == v1, lane, big), axis=-1, keepdims=True)
    el2 = jnp.where(lane == i1, -jnp.inf, el)
    v2 = jnp.max(el2, axis=-1, keepdims=True)
    i2 = jnp.min(jnp.where(el2 == v2, lane, big), axis=-1, keepdims=True)
    e21 = jnp.exp(v2 - v1)
    w1 = 1.0 / (1.0 + e21)
    w2 = e21 / (1.0 + e21)
    gate = p_top * (jnp.where(lane == i1, w1, 0.0) + jnp.where(lane == i2, w2, 0.0))
    gate_ref[...] = jnp.where(lane == N_EXPERTS, g_star.astype(F32), gate)


def _outproj(layer, x2d, oa, ob, dil, w, tm, b, s):
    t = x2d.shape[0]
    row = lambda n: pl.BlockSpec((tm, n), lambda i: (i, 0))
    wspec = lambda a: pl.BlockSpec((1,) + a.shape[1:], lambda i: (layer,) + (0,) * (a.ndim - 1))
    tiles_per_seq = s // tm
    res_major = lambda d: pl.BlockSpec((1, tm // d, d * 256), lambda i: (i // tiles_per_seq, i % tiles_per_seq, 0))
    d1, d2 = DIL_CONFIGS[1][1], DIL_CONFIGS[2][1]
    weights = [w["wo"], w["ln1_g"], w["ln1_b"], w["rwh"], w["rwl"], w["rb"]]
    est = (D_MODEL * D_MODEL * 2 * 2 + 2 * D_MODEL * LANES * 2 * 2 + tm * D_MODEL * 4 * 4
           + tm * (512 + 256) * 2 * 2 + 6 * tm * 256 * 4 * 2 + 4 * tm * 256 * 4 + tm * LANES * 4 * 2
           + 6 * tm * D_MODEL * 4)
    return pl.pallas_call(
        _outproj_kernel,
        grid=(t // tm,),
        in_specs=[row(D_MODEL), row(MLA_HEADS * MLA_V), row(256), row(256), row(256),
                  res_major(d1), res_major(d1), res_major(d2), res_major(d2)] + [wspec(a) for a in weights],
        out_specs=[row(D_MODEL), row(LANES)],
        out_shape=[jax.ShapeDtypeStruct((t, D_MODEL), F32), jax.ShapeDtypeStruct((t, LANES), F32)],
        scratch_shapes=[pltpu.VMEM((8, tm, LANES), F32)],
        compiler_params=pltpu.CompilerParams(dimension_semantics=("arbitrary",),
                                             vmem_limit_bytes=_vmem_limit(est)),
        name="outproj_ln_router",
    )(x2d, oa, ob, dil[0][0].reshape(t, 256), dil[0][1].reshape(t, 256),
      dil[1][0], dil[1][1], dil[2][0], dil[2][1], *weights)


def _moe_kernel(x_ref, gate_ref, wg_ref, wu_ref, wd_ref, g_ref, b_ref, o_ref, xb_scr, acc_scr, tri_scr, *, sub):
    g = pl.program_id(1)
    wt = x_ref.shape[0]

    @pl.when(g == 0)
    def _():
        xb_scr[...] = x_ref[...].astype(BF16)
        acc_scr[...] = jnp.zeros_like(acc_scr)

    gate = gate_ref[...]
    lane = lax.broadcasted_iota(jnp.int32, (wt, LANES), 1)
    g_star = jnp.sum(jnp.where(lane == N_EXPERTS, gate, 0.0), axis=-1, keepdims=True)
    member = g_star == g.astype(F32)
    memb = jnp.where(member, 1.0, 0.0)
    @pl.when((pl.program_id(0) == 0) & (g == 0))
    def _():
        before = lax.broadcasted_iota(jnp.int32, (wt, wt), 1) < lax.broadcasted_iota(jnp.int32, (wt, wt), 0)
        tri_scr[...] = jnp.where(before, 1.0, 0.0).astype(BF16)

    rank = _dot(tri_scr[...], jnp.broadcast_to(memb, (wt, LANES)).astype(BF16))
    rank = jnp.where(member, rank, -1.0)
    rank_row = jnp.transpose(rank)
    n_sub = (jnp.sum(memb).astype(jnp.int32) + (sub - 1)) // sub
    gate_hi = gate.astype(BF16)
    gate_lo = (gate - gate_hi.astype(F32)).astype(BF16)
    lane_sub = lax.broadcasted_iota(jnp.int32, (sub, LANES), 1)
    slot_rows = lax.broadcasted_iota(jnp.int32, (sub, wt), 0).astype(F32)
    slot_cols = lax.broadcasted_iota(jnp.int32, (wt, sub), 1).astype(F32)

    def sub_tile(j, carry):
        base = (j * sub).astype(F32)
        take = jnp.where(rank_row[0:1, :] == slot_rows + base, 1.0, 0.0).astype(BF16)
        xs = _dot(take, xb_scr[...]).astype(BF16)
        gs = _dot(take, gate_hi) + _dot(take, gate_lo)
        y = jnp.zeros((sub, D_MODEL), F32)
        for e in range(EXPERTS_PER_GROUP):
            a = _dot(xs, wg_ref[0, 0, e])
            u = _dot(xs, wu_ref[0, 0, e])
            ge = jnp.sum(jnp.where(lane_sub == g * EXPERTS_PER_GROUP + e, gs, 0.0), axis=-1, keepdims=True)
            h = (a / (1.0 + jnp.exp(-a))) * u * ge
            y = y + _dot(h.astype(BF16), wd_ref[0, 0, e])
        put = jnp.where(rank[:, 0:1] == slot_cols + base, 1.0, 0.0).astype(BF16)
        y_hi = y.astype(BF16)
        y_lo = (y - y_hi.astype(F32)).astype(BF16)
        acc_scr[...] += _dot(put, y_hi) + _dot(put, y_lo)
        return carry

    lax.fori_loop(0, n_sub, sub_tile, 0)

    @pl.when(g == pl.num_programs(1) - 1)
    def _():
        o_ref[...] = _layer_norm(DEEPNORM_ALPHA * x_ref[...] + acc_scr[...], g_ref[0], b_ref[0])


def _moe(layer, x1, gate, w, wt):
    t = x1.shape[0]
    f = EXPERT_HIDDEN
    epg = EXPERTS_PER_GROUP
    sub = min(wt, -(-(wt // N_GROUPS + wt // 32) // 16) * 16)
    est = (wt * D_MODEL * 4 * 4 + wt * LANES * 4 * 2 + 3 * epg * D_MODEL * f * 2 * 2
           + wt * D_MODEL * 6 + wt * wt * 2 + 6 * sub * D_MODEL * 4)
    wspec = lambda shape: pl.BlockSpec((1, 1) + shape, lambda i, g: (layer, g, 0, 0, 0))
    grouped = lambda a: a.reshape((a.shape[0], N_GROUPS, epg) + a.shape[2:])
    return pl.pallas_call(
        functools.partial(_moe_kernel, sub=sub),
        grid=(t // wt, N_GROUPS),
        in_specs=[pl.BlockSpec((wt, D_MODEL), lambda i, g: (i, 0)),
                  pl.BlockSpec((wt, LANES), lambda i, g: (i, 0)),
                  wspec((epg, D_MODEL, f)), wspec((epg, D_MODEL, f)), wspec((epg, f, D_MODEL)),
                  pl.BlockSpec((1, 1, D_MODEL), lambda i, g: (layer, 0, 0)),
                  pl.BlockSpec((1, 1, D_MODEL), lambda i, g: (layer, 0, 0))],
        out_specs=pl.BlockSpec((wt, D_MODEL), lambda i, g: (i, 0)),
        out_shape=jax.ShapeDtypeStruct((t, D_MODEL), F32),
        scratch_shapes=[pltpu.VMEM((wt, D_MODEL), BF16), pltpu.VMEM((wt, D_MODEL), F32),
                        pltpu.VMEM((wt, wt), BF16)],
        compiler_params=pltpu.CompilerParams(dimension_semantics=("arbitrary", "arbitrary"),
                                             vmem_limit_bytes=_vmem_limit(est)),
        name="moe_ln",
    )(x1, gate, grouped(w["wg"]), grouped(w["wu"]), grouped(w["wd"]), w["ln2_g"], w["ln2_b"])


def _pick_tile(n, pref):
    t = min(pref, n)
    while n % t:
        t //= 2
    return t


def _prepare_weights(w_in, mla_q_norm, mla_kv_norm, mla_w_uq, mla_w_ukv, w_o, ln1_g, ln1_b,
                     router_group_w, router_group_b, router_expert_w, router_expert_b,
                     expert_w_gate, expert_w_up, expert_w_down, ln2_g, ln2_b):
    depth = w_in.shape[0]
    w_perm = _gather_cols(w_in, _inproj_column_map()).astype(BF16)
    vec = lambda a: a.reshape(depth, 1, -1)
    rw = jnp.concatenate([jnp.moveaxis(router_expert_w, 1, 2).reshape(depth, D_MODEL, N_EXPERTS),
                          router_group_w,
                          jnp.zeros((depth, D_MODEL, LANES - N_EXPERTS - N_GROUPS), F32)], axis=-1)
    rb = jnp.concatenate([router_expert_b.reshape(depth, N_EXPERTS), router_group_b,
                          jnp.zeros((depth, LANES - N_EXPERTS - N_GROUPS), F32)], axis=-1)
    rwh = rw.astype(BF16)
    ex = lambda a: a.reshape((depth, N_EXPERTS) + a.shape[3:]).astype(BF16)
    return {
        "wa": w_perm[:, :, :_WA], "wb": w_perm[:, :, _WA:_WA + _WB], "wc": w_perm[:, :, _WA + _WB:],
        "wuq": _gather_cols(mla_w_uq, _uq_column_map()).astype(BF16),
        "wukv": _gather_cols(mla_w_ukv, _ukv_column_map()).astype(BF16),
        "qn": vec(mla_q_norm), "kvn": vec(mla_kv_norm),
        "wo": w_o.astype(BF16), "ln1_g": vec(ln1_g), "ln1_b": vec(ln1_b),
        "rwh": rwh, "rwl": (rw - rwh.astype(F32)).astype(BF16), "rb": vec(rb),
        "wg": ex(expert_w_gate), "wu": ex(expert_w_up), "wd": ex(expert_w_down),
        "ln2_g": vec(ln2_g), "ln2_b": vec(ln2_b),
    }


def kernel(x, positions, w_in, mla_q_norm, mla_kv_norm, mla_w_uq, mla_w_ukv, w_o, ln1_g, ln1_b,
           router_group_w, router_group_b, router_expert_w, router_expert_b,
           expert_w_gate, expert_w_up, expert_w_down, ln2_g, ln2_b):
    b, s, d_model = x.shape
    assert d_model == D_MODEL and w_in.shape[-1] == _IN_COLS
    t = b * s
    w = _prepare_weights(w_in, mla_q_norm, mla_kv_norm, mla_w_uq, mla_w_ukv, w_o, ln1_g, ln1_b,
                         router_group_w, router_group_b, router_expert_w, router_expert_b,
                         expert_w_gate, expert_w_up, expert_w_down, ln2_g, ln2_b)
    tabs = [a.reshape(t, LANES) for a in _rope_tables(positions)]
    tm_proj = _pick_tile(s, 512)
    tm_moe = _pick_tile(t, 1024)
    tq_mla = _pick_tile(s, 512)
    tq_dsa = _pick_tile(s, 256)
    n_cfg = len(DIL_CONFIGS)
    x2d = x.reshape(t, D_MODEL)
    for layer in range(w_in.shape[0]):
        (mq, mk, mv, dq, dk, dv, qi, ki, wi), dil_qkv = _inproj(layer, x2d, w, tabs, tm_proj, b, s)
        oa = _mla_attention(mq, mk, mv, b, s, tq_mla).reshape(t, -1)
        ob = _dsa_attention(dq, dk, dv, qi, ki, wi, b, s, tq_dsa).reshape(t, -1)
        dil = [_dilated_attention(dil_qkv[g], dil_qkv[n_cfg + g], dil_qkv[2 * n_cfg + g], g, b, s)
               for g in range(n_cfg)]
        x1, gate = _outproj(layer, x2d, oa, ob, dil, w, tm_proj, b, s)
        x2d = _moe(layer, x1, gate, w, tm_moe)
    return x2d.reshape(b, s, D_MODEL)
```

```python
import functools

import numpy as np
import jax
import jax.numpy as jnp
from jax import lax
from jax.experimental import pallas as pl
from jax.experimental.pallas import tpu as pltpu

D_MODEL = 1024
DEPTH = 4
HEAD_DIM = 64
MLA_HEADS = 8
MLA_Q_RANK = 256
MLA_KV_RANK = 128
MLA_NOPE = 64
MLA_ROPE = 32
MLA_V = 64
DSA_HEADS = 4
IDX_HEADS = 8
IDX_DIM = 32
DSA_TOPK_MAX = 256
DIL_CONFIGS = ((128, 1), (512, 4), (2048, 16))
DIL_HEADS = 4
ROPE_THETA = 500000.0
ROT_HEAD_DIMS = HEAD_DIM // 4
ROT_IDX_DIMS = IDX_DIM // 4
N_GROUPS = 4
EXPERTS_PER_GROUP = 8
N_EXPERTS = N_GROUPS * EXPERTS_PER_GROUP
EXPERT_HIDDEN = 256
DEEPNORM_ALPHA = (2.0 * DEPTH) ** 0.25
LN_EPS = 1e-5
RMS_EPS = 1e-6
NEG_INF = -1e30

LANES = 128
VMEM_CAP_BYTES = 60000 * 1024
INT_MIN = -(2 ** 31)

BF16 = jnp.bfloat16
F32 = jnp.float32

_OFF_CQ = 0
_OFF_CKV = _OFF_CQ + MLA_Q_RANK
_OFF_KPE = _OFF_CKV + MLA_KV_RANK
_OFF_BQ = _OFF_KPE + MLA_ROPE
_OFF_BK = _OFF_BQ + DSA_HEADS * HEAD_DIM
_OFF_BV = _OFF_BK + HEAD_DIM
_OFF_BQI = _OFF_BV + HEAD_DIM
_OFF_BKI = _OFF_BQI + IDX_HEADS * IDX_DIM
_OFF_BWI = _OFF_BKI + IDX_DIM
_OFF_C = _OFF_BWI + IDX_HEADS
_N_DIL = len(DIL_CONFIGS) * DIL_HEADS * HEAD_DIM
_IN_COLS = _OFF_C + 3 * _N_DIL

_WA = MLA_Q_RANK + MLA_KV_RANK + LANES
_WB = 5 * 256 + LANES
_WC = 3 * _N_DIL


def _vmem_limit(est_bytes):
    return int(min(max(2 * est_bytes, 32 * 1024 * 1024), VMEM_CAP_BYTES))


def _dot(a, b):
    return jnp.dot(a, b, preferred_element_type=F32)


def _dot_nt(a, b):
    return lax.dot_general(a, b, (((1,), (1,)), ((), ())), preferred_element_type=F32)


def _inproj_column_map():
    a = list(range(_OFF_CQ, _OFF_CQ + MLA_Q_RANK)) + list(range(_OFF_CKV, _OFF_CKV + MLA_KV_RANK))
    kpe = [-1] * LANES
    kpe[MLA_NOPE:MLA_NOPE + MLA_ROPE] = range(_OFF_KPE, _OFF_KPE + MLA_ROPE)
    a += kpe
    b = list(range(_OFF_BQ, _OFF_BQ + 256))
    b += list(range(_OFF_BK, _OFF_BK + HEAD_DIM)) * DSA_HEADS
    b += list(range(_OFF_BV, _OFF_BV + HEAD_DIM)) * DSA_HEADS
    b += list(range(_OFF_BQI, _OFF_BQI + 256))
    b += list(range(_OFF_BKI, _OFF_BKI + IDX_DIM)) * IDX_HEADS
    b += list(range(_OFF_BWI, _OFF_BWI + IDX_HEADS)) + [-1] * (LANES - IDX_HEADS)
    c = list(range(_OFF_C, _OFF_C + _WC))
    assert len(a) == _WA and len(b) == _WB and len(c) == _WC
    return np.asarray(a + b + c, np.int32)


def _gather_cols(w, cols):
    cols = np.asarray(cols, np.int32)
    g = jnp.take(w, jnp.asarray(np.maximum(cols, 0)), axis=-1)
    return jnp.where(jnp.asarray(cols >= 0), g, 0.0)


def _uq_column_map():
    cols = []
    for h in range(MLA_HEADS):
        base = h * (MLA_NOPE + MLA_ROPE)
        cols += list(range(base, base + MLA_NOPE + MLA_ROPE)) + [-1] * (LANES - MLA_NOPE - MLA_ROPE)
    return cols


def _ukv_column_map():
    kcols, vcols = [], []
    for h in range(MLA_HEADS):
        base = h * (MLA_NOPE + MLA_V)
        kcols += list(range(base, base + MLA_NOPE)) + [-1] * (LANES - MLA_NOPE)
        vcols += list(range(base + MLA_NOPE, base + MLA_NOPE + MLA_V)) + [-1] * (LANES - MLA_V)
    return kcols + vcols


def _rope_tables(positions):
    pos = positions.astype(F32)[..., None]

    def cs(rot_dims):
        inv = ROPE_THETA ** (-jnp.arange(0, rot_dims, 2, dtype=F32) / rot_dims)
        ang = pos * inv
        return jnp.cos(ang), jnp.sin(ang)

    def pattern(c, s, period, start):
        half = c.shape[-1]
        one = jnp.ones(c.shape[:-1] + (period,), F32)
        zero = jnp.zeros_like(one)
        cc = one.at[..., start:start + half].set(c).at[..., start + half:start + 2 * half].set(c)
        ss = zero.at[..., start:start + half].set(-s).at[..., start + half:start + 2 * half].set(s)
        reps = LANES // period
        return jnp.tile(cc, (1, 1, reps)), jnp.tile(ss, (1, 1, reps))

    cm, sm = pattern(*cs(MLA_ROPE), LANES, MLA_NOPE)
    ch, sh = pattern(*cs(ROT_HEAD_DIMS), HEAD_DIM, 0)
    ci, si = pattern(*cs(ROT_IDX_DIMS), IDX_DIM, 0)
    return cm, sm, ch, sh, ci, si


def _rope_block(u, c, s, first_half, half):
    rot = jnp.where(first_half, pltpu.roll(u, LANES - half, 1), pltpu.roll(u, half, 1))
    return u * c + rot * s


def _inproj_kernel(x_ref, wa_ref, wb_ref, wc_ref, wuq_ref, wukv_ref, qn_ref, kvn_ref,
                   cm_ref, sm_ref, ch_ref, sh_ref, ci_ref, si_ref,
                   mq_ref, mk_ref, mv_ref, dq_ref, dk_ref, dv_ref, qi_ref, ki_ref, wi_ref,
                   *dil_refs_and_scratch):
    dil_refs, perm_scr = dil_refs_and_scratch[:-1], dil_refs_and_scratch[-1]
    tm = x_ref.shape[0]
    lane = lax.broadcasted_iota(jnp.int32, (tm, LANES), 1)
    first_m = (lane >= MLA_NOPE) & (lane < MLA_NOPE + MLA_ROPE // 2)
    first_h = (lane & (HEAD_DIM - 1)) < ROT_HEAD_DIMS // 2
    first_i = (lane & (IDX_DIM - 1)) < ROT_IDX_DIMS // 2
    cm, sm = cm_ref[...], sm_ref[...]
    ch, sh = ch_ref[...], sh_ref[...]
    ci, si = ci_ref[...], si_ref[...]
    rope_m = functools.partial(_rope_block, c=cm, s=sm, first_half=first_m, half=MLA_ROPE // 2)
    rope_h = functools.partial(_rope_block, c=ch, s=sh, first_half=first_h, half=ROT_HEAD_DIMS // 2)
    rope_i = functools.partial(_rope_block, c=ci, s=si, first_half=first_i, half=ROT_IDX_DIMS // 2)

    xb = x_ref[...].astype(BF16)

    ua = _dot(xb, wa_ref[0])
    cq = ua[:, :MLA_Q_RANK]
    cqn = cq * lax.rsqrt(jnp.mean(cq * cq, axis=-1, keepdims=True) + RMS_EPS) * qn_ref[0]
    q = _dot(cqn.astype(BF16), wuq_ref[0])
    mla_scale = (MLA_NOPE + MLA_ROPE) ** -0.5
    for h in range(MLA_HEADS):
        sl = slice(h * LANES, (h + 1) * LANES)
        mq_ref[:, sl] = (rope_m(q[:, sl]) * mla_scale).astype(BF16)
    ckv = ua[:, MLA_Q_RANK:MLA_Q_RANK + MLA_KV_RANK]
    ckvn = ckv * lax.rsqrt(jnp.mean(ckv * ckv, axis=-1, keepdims=True) + RMS_EPS) * kvn_ref[0]
    kv = _dot(ckvn.astype(BF16), wukv_ref[0])
    kpe = rope_m(ua[:, MLA_Q_RANK + MLA_KV_RANK:])
    value_lanes = lane < HEAD_DIM
    for h in range(MLA_HEADS):
        sl = slice(h * LANES, (h + 1) * LANES)
        mk_ref[:, sl] = (kv[:, sl] + kpe).astype(BF16)
        v_blk = kv[:, (MLA_HEADS + h) * LANES:(MLA_HEADS + h + 1) * LANES]
        mv_ref[:, sl] = jnp.where(value_lanes, v_blk, 1.0).astype(BF16)

    ub = _dot(xb, wb_ref[0])
    dsa_scale = HEAD_DIM ** -0.5
    idx_scale = IDX_DIM ** -0.5 * IDX_HEADS ** -0.5
    for j in range(2):
        sl = slice(j * LANES, (j + 1) * LANES)
        dq_ref[:, sl] = (rope_h(ub[:, sl]) * dsa_scale).astype(BF16)
        dk_ref[:, sl] = rope_h(ub[:, 256 + j * LANES:256 + (j + 1) * LANES]).astype(BF16)
        qi_ref[:, sl] = rope_i(ub[:, 768 + j * LANES:768 + (j + 1) * LANES]).astype(BF16)
        ki_ref[:, sl] = rope_i(ub[:, 1024 + j * LANES:1024 + (j + 1) * LANES]).astype(BF16)
    dv_ref[...] = jnp.where(value_lanes, ub[:, 512:512 + LANES], 1.0).astype(BF16)
    wi_ref[...] = ub[:, 1280:] * idx_scale

    uc = _dot(xb, wc_ref[0])
    n_cfg = len(DIL_CONFIGS)
    for which in range(3):
        for g, (_, d) in enumerate(DIL_CONFIGS):
            halves = []
            for j in range(2):
                c0 = which * _N_DIL + g * 256 + j * LANES
                blk = uc[:, c0:c0 + LANES]
                if which < 2:
                    blk = rope_h(blk)
                if which == 0:
                    blk = blk * dsa_scale
                halves.append(blk)
            out_ref = dil_refs[which * n_cfg + g]
            if d == 1:
                out_ref[:, :LANES] = halves[0].astype(BF16)
                out_ref[:, LANES:] = halves[1].astype(BF16)
            else:
                for j in range(2):
                    slot = 2 * (which * (n_cfg - 1) + g - 1) + j
                    perm_scr[slot] = halves[j]
                    for r in range(d):
                        rows = perm_scr[slot, pl.ds(r, tm // d, stride=d), :]
                        out_ref[0, :, r * 256 + j * LANES:r * 256 + (j + 1) * LANES] = rows.astype(BF16)


def _inproj(layer, x2d, w, tabs, tm, b, s):
    t = x2d.shape[0]
    row = lambda n: pl.BlockSpec((tm, n), lambda i: (i, 0))
    wspec = lambda a: pl.BlockSpec((1,) + a.shape[1:], lambda i: (layer,) + (0,) * (a.ndim - 1))
    outs = [(8 * LANES, BF16), (8 * LANES, BF16), (8 * LANES, BF16),
            (256, BF16), (256, BF16), (LANES, BF16), (256, BF16), (256, BF16), (LANES, F32)]
    weights = [w["wa"], w["wb"], w["wc"], w["wuq"], w["wukv"], w["qn"], w["kvn"]]
    tiles_per_seq = s // tm
    dil_specs, dil_shapes = [], []
    for _ in range(3):
        for _, d in DIL_CONFIGS:
            if d == 1:
                dil_specs.append(row(256))
                dil_shapes.append(jax.ShapeDtypeStruct((t, 256), BF16))
            else:
                dil_specs.append(pl.BlockSpec((1, tm // d, d * 256),
                                              lambda i: (i // tiles_per_seq, i % tiles_per_seq, 0)))
                dil_shapes.append(jax.ShapeDtypeStruct((b, s // d, d * 256), BF16))
    est = (sum(int(np.prod(a.shape[1:])) * a.dtype.itemsize for a in weights) * 2
           + tm * D_MODEL * 4 * 2 + 6 * tm * LANES * 4 * 2
           + (sum(tm * n * jnp.dtype(d).itemsize for n, d in outs) + 9 * tm * 256 * 2) * 2
           + 6 * tm * 256 * 4 + tm * (_WA + _WB + _WC + 8 * LANES + 12 * LANES) * 4)
    res = pl.pallas_call(
        _inproj_kernel,
        grid=(t // tm,),
        in_specs=[row(D_MODEL)] + [wspec(a) for a in weights] + [row(LANES)] * 6,
        out_specs=[row(n) for n, _ in outs] + dil_specs,
        out_shape=[jax.ShapeDtypeStruct((t, n), d) for n, d in outs] + dil_shapes,
        scratch_shapes=[pltpu.VMEM((2 * 3 * (len(DIL_CONFIGS) - 1), tm, LANES), F32)],
        compiler_params=pltpu.CompilerParams(dimension_semantics=("arbitrary",),
                                             vmem_limit_bytes=_vmem_limit(est)),
        name="inproj",
    )(x2d, *weights, *tabs)
    return res[:len(outs)], res[len(outs):]


def _softmax_updates(scores, values, states):
    ps, partial = [], []
    for s, (m_prev, acc_prev) in zip(scores, states):
        m_new = jnp.maximum(m_prev, jnp.max(s, axis=-1, keepdims=True))
        ps.append(jnp.exp(s - m_new).astype(BF16))
        partial.append((m_new, jnp.exp(m_prev - m_new) * acc_prev))
    return tuple((m, acc + _dot(p, v)) for (m, acc), p, v in zip(partial, ps, values))


def _softmax_init(rows, n):
    return tuple((jnp.full((rows, 1), NEG_INF, F32), jnp.zeros((rows, LANES), F32)) for _ in range(n))


def _normalized_pair(state_even, state_odd):
    (_, acc_even), (_, acc_odd) = state_even, state_odd
    lane = lax.broadcasted_iota(jnp.int32, acc_even.shape, 1)
    o_even = acc_even / acc_even[:, HEAD_DIM:HEAD_DIM + 1]
    o_odd = acc_odd / acc_odd[:, HEAD_DIM:HEAD_DIM + 1]
    return jnp.where(lane < HEAD_DIM, o_even, pltpu.roll(o_odd, HEAD_DIM, 1))


def _mla_kernel(q_ref, k_ref, v_ref, o_ref, *, tq, nh):
    qi = pl.program_id(2)
    row = lax.broadcasted_iota(jnp.int32, (tq, tq), 0)
    col = lax.broadcasted_iota(jnp.int32, (tq, tq), 1)

    def step(j, states, diagonal):
        start = pl.multiple_of(j * tq, tq)
        scores, values = [], []
        for h in range(nh):
            hs = slice(h * LANES, (h + 1) * LANES)
            s = _dot_nt(q_ref[0, :, hs], k_ref[0, pl.ds(start, tq), hs])
            scores.append(jnp.where(col <= row, s, NEG_INF) if diagonal else s)
            values.append(v_ref[0, pl.ds(start, tq), hs])
        return _softmax_updates(scores, values, states)

    states = lax.fori_loop(0, qi, lambda j, st: step(j, st, False), _softmax_init(tq, nh))
    states = step(qi, states, True)
    for pair in range(nh // 2):
        o_ref[0, :, pair * LANES:(pair + 1) * LANES] = _normalized_pair(
            states[2 * pair], states[2 * pair + 1]).astype(BF16)


def _mla_attention(mq, mk, mv, b, s, tq, nh=4):
    est = (tq * nh * LANES * 2 * 2 + 2 * s * nh * LANES * 2 * 2 + tq * nh * MLA_V * 2 * 2
           + nh * tq * 3 * LANES * 4 + 4 * nh * tq * tq * 4)
    return pl.pallas_call(
        functools.partial(_mla_kernel, tq=tq, nh=nh),
        grid=(b, MLA_HEADS // nh, s // tq),
        in_specs=[pl.BlockSpec((1, tq, nh * LANES), lambda bi, h, i: (bi, i, h)),
                  pl.BlockSpec((1, s, nh * LANES), lambda bi, h, i: (bi, 0, h)),
                  pl.BlockSpec((1, s, nh * LANES), lambda bi, h, i: (bi, 0, h))],
        out_specs=pl.BlockSpec((1, tq, nh * MLA_V), lambda bi, h, i: (bi, i, h)),
        out_shape=jax.ShapeDtypeStruct((b, s, MLA_HEADS * MLA_V), BF16),
        compiler_params=pltpu.CompilerParams(dimension_semantics=("arbitrary",) * 3,
                                             vmem_limit_bytes=_vmem_limit(est)),
        name="mla_attn",
    )(mq.reshape(b, s, -1), mk.reshape(b, s, -1), mv.reshape(b, s, -1))


def _fold_lanes(x):
    acc = x[:, :LANES]
    for part in range(1, x.shape[1] // LANES):
        acc = acc + x[:, part * LANES:(part + 1) * LANES]
    return acc


def _dsa_kernel(q_ref, k_ref, v_ref, qi_ref, ki_ref, wi_ref, o_ref,
                key_scr, sel_scr, qim_scr, wb_scr, qm_scr, thr_scr, *, tq, top_k):
    i = pl.program_id(1)
    n = i + 1
    lane = lax.broadcasted_iota(jnp.int32, (tq, 256), 1)
    row = lax.broadcasted_iota(jnp.int32, (tq, tq), 0)
    col = lax.broadcasted_iota(jnp.int32, (tq, tq), 1)

    def causal(c):
        return col <= row + (i - c) * tq

    qi_all = qi_ref[0]
    wi = wi_ref[0]
    for h in range(IDX_HEADS):
        qim_scr[h] = jnp.where((lane >> 5) == h, qi_all, jnp.zeros_like(qi_all))
        wb_scr[h] = jnp.broadcast_to(wi[:, h:h + 1], (tq, LANES))
    q_all = q_ref[0]
    for h in range(DSA_HEADS):
        qm_scr[h] = jnp.where((lane >> 6) == h, q_all, jnp.zeros_like(q_all))

    def score_chunk(c, carry):
        start = pl.multiple_of(c * tq, tq)
        ki_c = ki_ref[0, pl.ds(start, tq), :]
        sc = jnp.zeros((tq, tq), F32)
        for h in range(IDX_HEADS):
            wb = wb_scr[h]
            sc = sc + jnp.maximum(_dot_nt(qim_scr[h], ki_c), 0.0) * jnp.concatenate([wb] * (tq // LANES), axis=1)
        sc = jnp.where(causal(c), sc, NEG_INF) + 0.0
        bits = lax.bitcast_convert_type(sc, jnp.int32)
        key_scr[c] = bits ^ ((bits >> 31) & jnp.int32(0x7FFFFFFF))
        return carry

    lax.fori_loop(0, n, score_chunk, 0)

    kf = jnp.float32(top_k)

    def bisect_over(n_chunks):
        def count_ge(cand):
            acc = jnp.zeros((tq, LANES), F32)
            for c in range(n_chunks):
                acc = acc + _fold_lanes(jnp.where(key_scr[c] >= cand, 1.0, 0.0))
            return jnp.sum(acc, axis=-1, keepdims=True)

        t0 = jnp.where(count_ge(jnp.zeros((tq, 1), jnp.int32)) >= kf, jnp.int32(0), jnp.int32(INT_MIN))

        def bisect(b, t):
            cand = t | lax.shift_left(jnp.int32(1), 30 - b)
            return jnp.where(count_ge(cand) >= kf, cand, t)

        thr_scr[...] = lax.fori_loop(0, 31, bisect, t0, unroll=4)

    for n_chunks in range(1, key_scr.shape[0] + 1):
        pl.when(n == n_chunks)(functools.partial(bisect_over, n_chunks))
    thr = thr_scr[...]

    def stats_chunk(c, carry):
        n_gt, n_eq, n_eqv = carry
        key = key_scr[c]
        eq = key == thr
        sel_scr[c] = jnp.where(key >= thr, 1.0, 0.0)
        return (n_gt + _fold_lanes(jnp.where(key > thr, 1.0, 0.0)),
                n_eq + _fold_lanes(jnp.where(eq, 1.0, 0.0)),
                n_eqv + _fold_lanes(jnp.where(eq & causal(c), 1.0, 0.0)))

    zeros = jnp.zeros((tq, LANES), F32)
    n_gt, n_eq, n_eqv = [jnp.sum(a, axis=-1, keepdims=True)
                         for a in lax.fori_loop(0, n, stats_chunk, (zeros, zeros, zeros))]
    room = kf - n_gt
    need_rank = jnp.max(jnp.where((n_eq > room) & (n_eqv > 0.0), 1.0, 0.0)) > 0.0

    @pl.when(need_rank)
    def _():
        upper = jnp.where(row < col, 1.0, 0.0).astype(BF16)

        def rank_chunk(c, before):
            key_c = key_scr[c]
            eq_c = jnp.where(key_c == thr, 1.0, 0.0)
            rank = _dot(eq_c.astype(BF16), upper) + before
            keep = (key_c > thr) | ((key_c == thr) & (rank < room))
            sel_scr[c] = jnp.where(keep, 1.0, 0.0)
            return before + jnp.sum(eq_c, axis=-1, keepdims=True)

        lax.fori_loop(0, n, rank_chunk, jnp.zeros((tq, 1), F32))

    def attn_chunk(c, states):
        start = pl.multiple_of(c * tq, tq)
        k_c = k_ref[0, pl.ds(start, tq), :]
        v_c = v_ref[0, pl.ds(start, tq), :]
        ok = (sel_scr[c] > 0.0) & causal(c)
        scores = [jnp.where(ok, _dot_nt(qm_scr[h], k_c), NEG_INF) for h in range(DSA_HEADS)]
        return _softmax_updates(scores, [v_c] * DSA_HEADS, states)

    states = lax.fori_loop(0, n, attn_chunk, _softmax_init(tq, DSA_HEADS))
    for pair in range(DSA_HEADS // 2):
        o_ref[0, :, pair * LANES:(pair + 1) * LANES] = _normalized_pair(
            states[2 * pair], states[2 * pair + 1]).astype(BF16)


def _dsa_attention(dq, dk, dv, qi, ki, wi, b, s, tq):
    top_k = min(DSA_TOPK_MAX, s // 4)
    assert tq >= top_k and tq % LANES == 0
    qspec = lambda n: pl.BlockSpec((1, tq, n), lambda bi, i: (bi, i, 0))
    kspec = lambda n: pl.BlockSpec((1, s, n), lambda bi, i: (bi, 0, 0))
    nc = s // tq
    scratch = [pltpu.VMEM((nc, tq, tq), jnp.int32), pltpu.VMEM((nc, tq, tq), F32),
               pltpu.VMEM((IDX_HEADS, tq, 256), BF16), pltpu.VMEM((IDX_HEADS, tq, LANES), F32),
               pltpu.VMEM((DSA_HEADS, tq, 256), BF16), pltpu.VMEM((tq, 1), jnp.int32)]
    est = (3 * tq * 256 * 2 * 2 + 3 * s * 256 * 2 * 2 + tq * LANES * 4 * 2 + 2 * tq * s * 4
           + 12 * tq * 256 * 2 + 8 * tq * LANES * 4 + 6 * tq * 256 * 4 + 16 * tq * tq * 4)
    r3 = lambda a: a.reshape(b, s, -1)
    return pl.pallas_call(
        functools.partial(_dsa_kernel, tq=tq, top_k=top_k),
        grid=(b, nc),
        in_specs=[qspec(256), kspec(256), kspec(LANES), qspec(256), kspec(256), qspec(LANES)],
        out_specs=qspec(256),
        out_shape=jax.ShapeDtypeStruct((b, s, 256), BF16),
        scratch_shapes=scratch,
        compiler_params=pltpu.CompilerParams(dimension_semantics=("arbitrary",) * 2,
                                             vmem_limit_bytes=_vmem_limit(est)),
        name="dsa_attn",
    )(r3(dq), r3(dk), r3(dv), r3(qi), r3(ki), r3(wi))


def _dilated_kernel(q_ref, kp_ref, kc_ref, vp_ref, vc_ref, o_ref, lse_ref, *, w, nrb, nres):
    j = pl.program_id(2)
    lane = lax.broadcasted_iota(jnp.int32, (w, 256), 1)
    row = lax.broadcasted_iota(jnp.int32, (w, 2 * w), 0)
    col = lax.broadcasted_iota(jnp.int32, (w, 2 * w), 1)
    rel = row + w - col
    in_window = (rel >= 0) & (rel <= w)
    first_rows_ok = in_window & (col + j * (2 * w) >= w)
    heads = [(lane >> 6) == h for h in range(DIL_HEADS)]
    units = []
    for rs in range(nres):
        cs = slice(rs * 256, (rs + 1) * 256)
        for rb in range(nrb):
            rows = slice(rb * w, (rb + 1) * w)
            q_all = q_ref[0, rows, cs]
            if rb == 0:
                k_prev, v_prev, ok = kp_ref[0, :, cs], vp_ref[0, :, cs], first_rows_ok
            else:
                prev_rows = slice((rb - 1) * w, rb * w)
                k_prev, v_prev, ok = kc_ref[0, prev_rows, cs], vc_ref[0, prev_rows, cs], in_window
            k = jnp.concatenate([k_prev, kc_ref[0, rows, cs]], axis=0)
            v = jnp.concatenate([v_prev, vc_ref[0, rows, cs]], axis=0)
            scores = [jnp.where(ok, _dot_nt(jnp.where(head, q_all, jnp.zeros_like(q_all)), k), NEG_INF)
                      for head in heads]
            units.append((rows, cs, v, scores))
    soft = []
    for rows, cs, v, scores in units:
        stats = []
        for sc in scores:
            m = jnp.max(sc, axis=-1, keepdims=True)
            p = jnp.exp(sc - m)
            stats.append((m, jnp.sum(p, axis=-1, keepdims=True), p.astype(BF16)))
        soft.append((rows, cs, v, stats))
    for rows, cs, v, stats in soft:
        out = jnp.zeros((w, 256), F32)
        lse = jnp.zeros((w, 256), F32)
        for head, (m, l, p) in zip(heads, stats):
            out = jnp.where(head, _dot(p, v) / l, out)
            lse = jnp.where(head, m + jnp.log(l), lse)
        o_ref[0, rows, cs] = out
        lse_ref[0, rows, cs] = lse


def _dilated_attention(q, k, v, g, b, s):
    window, d = DIL_CONFIGS[g]
    w = window // d
    steps = s // d
    assert steps % w == 0
    nrb = min(4, steps // w)
    nres = min(d, max(1, 4 // nrb))
    q, k, v = [a.reshape(b, steps, d * 256) for a in (q, k, v)]
    cur = pl.BlockSpec((1, nrb * w, nres * 256), lambda bi, c, j: (bi, j, c))
    prev = pl.BlockSpec((1, w, nres * 256), lambda bi, c, j: (bi, jnp.maximum(j * nrb - 1, 0), c))
    est = (3 * nrb * w * nres * 256 * 2 * 2 + 2 * w * nres * 256 * 2 * 2 + 2 * nrb * w * nres * 256 * 4 * 2
           + 16 * 8 * w * 2 * w * 4)
    return pl.pallas_call(
        functools.partial(_dilated_kernel, w=w, nrb=nrb, nres=nres),
        grid=(b, d // nres, steps // (nrb * w)),
        in_specs=[cur, prev, cur, prev, cur],
        out_specs=[cur, cur],
        out_shape=[jax.ShapeDtypeStruct((b, steps, d * 256), F32)] * 2,
        compiler_params=pltpu.CompilerParams(dimension_semantics=("arbitrary",) * 3,
                                             vmem_limit_bytes=_vmem_limit(est)),
        name=f"dilated_attn_{g}",
    )(q, k, k, v, v)


def _layer_norm(y, g, b):
    mu = jnp.mean(y, axis=-1, keepdims=True)
    yc = y - mu
    var = jnp.mean(yc * yc, axis=-1, keepdims=True)
    return yc * lax.rsqrt(var + LN_EPS) * g + b


def _outproj_kernel(x_ref, oa_ref, ob_ref, o0_ref, l0_ref, o1_ref, l1_ref, o2_ref, l2_ref,
                    wo_ref, g_ref, b_ref, rwh_ref, rwl_ref, rb_ref, x1_ref, gate_ref, tok_scr):
    tm = x_ref.shape[0]

    def token_major(ref, slot, d):
        for j in range(2):
            for r in range(d):
                c0 = r * 256 + j * LANES
                tok_scr[2 * slot + j, pl.ds(r, tm // d, stride=d), :] = ref[0, :, c0:c0 + LANES]
        return jnp.concatenate([tok_scr[2 * slot], tok_scr[2 * slot + 1]], axis=1)

    d1, d2 = DIL_CONFIGS[1][1], DIL_CONFIGS[2][1]
    o0, l0 = o0_ref[...], l0_ref[...]
    o1, l1 = token_major(o1_ref, 0, d1), token_major(l1_ref, 1, d1)
    o2, l2 = token_major(o2_ref, 2, d2), token_major(l2_ref, 3, d2)
    mx = jnp.maximum(jnp.maximum(l0, l1), l2)
    e0, e1, e2 = jnp.exp(l0 - mx), jnp.exp(l1 - mx), jnp.exp(l2 - mx)
    oc = (e0 * o0 + e1 * o1 + e2 * o2) / (e0 + e1 + e2)
    na = MLA_HEADS * MLA_V
    mix = (_dot(oa_ref[...], wo_ref[0, :na, :]) + _dot(ob_ref[...], wo_ref[0, na:na + 256, :])
           + _dot(oc.astype(BF16), wo_ref[0, na + 256:, :]))
    x1 = _layer_norm(DEEPNORM_ALPHA * x_ref[...] + mix, g_ref[0], b_ref[0])
    x1_ref[...] = x1

    xh = x1.astype(BF16)
    xl = (x1 - xh.astype(F32)).astype(BF16)
    logits = _dot(xh, rwh_ref[0]) + _dot(xl, rwh_ref[0]) + _dot(xh, rwl_ref[0]) + rb_ref[0]
    lane = lax.broadcasted_iota(jnp.int32, (tm, LANES), 1)
    big = jnp.int32(LANES)
    is_group = (lane >= N_EXPERTS) & (lane < N_EXPERTS + N_GROUPS)
    gl = jnp.where(is_group, logits, -jnp.inf)
    gmax = jnp.max(gl, axis=-1, keepdims=True)
    g_star = jnp.min(jnp.where(gl == gmax, lane - N_EXPERTS, big), axis=-1, keepdims=True)
    p_top = 1.0 / jnp.sum(jnp.exp(gl - gmax), axis=-1, keepdims=True)
    in_group = (lane < N_EXPERTS) & ((lane >> 3) == g_star)
    el = jnp.where(in_group, logits, -jnp.inf)
    v1 = jnp.max(el, axis=-1, keepdims=True)
    i1 = jnp.min(jnp.where(el == v1, lane, big), axis=-1, keepdims=True)
    el2 = jnp.where(lane == i1, -jnp.inf, el)
    v2 = jnp.max(el2, axis=-1, keepdims=True)
    i2 = jnp.min(jnp.where(el2 == v2, lane, big), axis=-1, keepdims=True)
    e21 = jnp.exp(v2 - v1)
    w1 = 1.0 / (1.0 + e21)
    w2 = e21 / (1.0 + e21)
    gate = p_top * (jnp.where(lane == i1, w1, 0.0) + jnp.where(lane == i2, w2, 0.0))
    gate_ref[...] = jnp.where(lane == N_EXPERTS, g_star.astype(F32), gate)


def _outproj(layer, x2d, oa, ob, dil, w, tm, b, s):
    t = x2d.shape[0]
    row = lambda n: pl.BlockSpec((tm, n), lambda i: (i, 0))
    wspec = lambda a: pl.BlockSpec((1,) + a.shape[1:], lambda i: (layer,) + (0,) * (a.ndim - 1))
    tiles_per_seq = s // tm
    res_major = lambda d: pl.BlockSpec((1, tm // d, d * 256), lambda i: (i // tiles_per_seq, i % tiles_per_seq, 0))
    d1, d2 = DIL_CONFIGS[1][1], DIL_CONFIGS[2][1]
    weights = [w["wo"], w["ln1_g"], w["ln1_b"], w["rwh"], w["rwl"], w["rb"]]
    est = (D_MODEL * D_MODEL * 2 * 2 + 2 * D_MODEL * LANES * 2 * 2 + tm * D_MODEL * 4 * 4
           + tm * (512 + 256) * 2 * 2 + 6 * tm * 256 * 4 * 2 + 4 * tm * 256 * 4 + tm * LANES * 4 * 2
           + 6 * tm * D_MODEL * 4)
    return pl.pallas_call(
        _outproj_kernel,
        grid=(t // tm,),
        in_specs=[row(D_MODEL), row(MLA_HEADS * MLA_V), row(256), row(256), row(256),
                  res_major(d1), res_major(d1), res_major(d2), res_major(d2)] + [wspec(a) for a in weights],
        out_specs=[row(D_MODEL), row(LANES)],
        out_shape=[jax.ShapeDtypeStruct((t, D_MODEL), F32), jax.ShapeDtypeStruct((t, LANES), F32)],
        scratch_shapes=[pltpu.VMEM((8, tm, LANES), F32)],
        compiler_params=pltpu.CompilerParams(dimension_semantics=("arbitrary",),
                                             vmem_limit_bytes=_vmem_limit(est)),
        name="outproj_ln_router",
    )(x2d, oa, ob, dil[0][0].reshape(t, 256), dil[0][1].reshape(t, 256),
      dil[1][0], dil[1][1], dil[2][0], dil[2][1], *weights)


def _moe_kernel(x_ref, gate_ref, wg_ref, wu_ref, wd_ref, g_ref, b_ref, o_ref, xb_scr, acc_scr, tri_scr, *, sub):
    g = pl.program_id(1)
    wt = x_ref.shape[0]

    @pl.when(g == 0)
    def _():
        xb_scr[...] = x_ref[...].astype(BF16)
        acc_scr[...] = jnp.zeros_like(acc_scr)

    gate = gate_ref[...]
    lane = lax.broadcasted_iota(jnp.int32, (wt, LANES), 1)
    g_star = jnp.sum(jnp.where(lane == N_EXPERTS, gate, 0.0), axis=-1, keepdims=True)
    member = g_star == g.astype(F32)
    memb = jnp.where(member, 1.0, 0.0)
    @pl.when((pl.program_id(0) == 0) & (g == 0))
    def _():
        before = lax.broadcasted_iota(jnp.int32, (wt, wt), 1) < lax.broadcasted_iota(jnp.int32, (wt, wt), 0)
        tri_scr[...] = jnp.where(before, 1.0, 0.0).astype(BF16)

    rank = _dot(tri_scr[...], jnp.broadcast_to(memb, (wt, LANES)).astype(BF16))
    rank = jnp.where(member, rank, -1.0)
    rank_row = jnp.transpose(rank)
    n_sub = (jnp.sum(memb).astype(jnp.int32) + (sub - 1)) // sub
    gate_hi = gate.astype(BF16)
    gate_lo = (gate - gate_hi.astype(F32)).astype(BF16)
    lane_sub = lax.broadcasted_iota(jnp.int32, (sub, LANES), 1)
    slot_rows = lax.broadcasted_iota(jnp.int32, (sub, wt), 0).astype(F32)
    slot_cols = lax.broadcasted_iota(jnp.int32, (wt, sub), 1).astype(F32)

    def sub_tile(j, carry):
        base = (j * sub).astype(F32)
        take = jnp.where(rank_row[0:1, :] == slot_rows + base, 1.0, 0.0).astype(BF16)
        xs = _dot(take, xb_scr[...]).astype(BF16)
        gs = _dot(take, gate_hi) + _dot(take, gate_lo)
        y = jnp.zeros((sub, D_MODEL), F32)
        for e in range(EXPERTS_PER_GROUP):
            a = _dot(xs, wg_ref[0, 0, e])
            u = _dot(xs, wu_ref[0, 0, e])
            ge = jnp.sum(jnp.where(lane_sub == g * EXPERTS_PER_GROUP + e, gs, 0.0), axis=-1, keepdims=True)
            h = (a / (1.0 + jnp.exp(-a))) * u * ge
            y = y + _dot(h.astype(BF16), wd_ref[0, 0, e])
        put = jnp.where(rank[:, 0:1] == slot_cols + base, 1.0, 0.0).astype(BF16)
        y_hi = y.astype(BF16)
        y_lo = (y - y_hi.astype(F32)).astype(BF16)
        acc_scr[...] += _dot(put, y_hi) + _dot(put, y_lo)
        return carry

    lax.fori_loop(0, n_sub, sub_tile, 0)

    @pl.when(g == pl.num_programs(1) - 1)
    def _():
        o_ref[...] = _layer_norm(DEEPNORM_ALPHA * x_ref[...] + acc_scr[...], g_ref[0], b_ref[0])


def _moe(layer, x1, gate, w, wt):
    t = x1.shape[0]
    f = EXPERT_HIDDEN
    epg = EXPERTS_PER_GROUP
    sub = min(wt, -(-(wt // N_GROUPS + wt // 32) // 16) * 16)
    est = (wt * D_MODEL * 4 * 4 + wt * LANES * 4 * 2 + 3 * epg * D_MODEL * f * 2 * 2
           + wt * D_MODEL * 6 + wt * wt * 2 + 6 * sub * D_MODEL * 4)
    wspec = lambda shape: pl.BlockSpec((1, 1) + shape, lambda i, g: (layer, g, 0, 0, 0))
    grouped = lambda a: a.reshape((a.shape[0], N_GROUPS, epg) + a.shape[2:])
    return pl.pallas_call(
        functools.partial(_moe_kernel, sub=sub),
        grid=(t // wt, N_GROUPS),
        in_specs=[pl.BlockSpec((wt, D_MODEL), lambda i, g: (i, 0)),
                  pl.BlockSpec((wt, LANES), lambda i, g: (i, 0)),
                  wspec((epg, D_MODEL, f)), wspec((epg, D_MODEL, f)), wspec((epg, f, D_MODEL)),
                  pl.BlockSpec((1, 1, D_MODEL), lambda i, g: (layer, 0, 0)),
                  pl.BlockSpec((1, 1, D_MODEL), lambda i, g: (layer, 0, 0))],
        out_specs=pl.BlockSpec((wt, D_MODEL), lambda i, g: (i, 0)),
        out_shape=jax.ShapeDtypeStruct((t, D_MODEL), F32),
        scratch_shapes=[pltpu.VMEM((wt, D_MODEL), BF16), pltpu.VMEM((wt, D_MODEL), F32),
                        pltpu.VMEM((wt, wt), BF16)],
        compiler_params=pltpu.CompilerParams(dimension_semantics=("arbitrary", "arbitrary"),
                                             vmem_limit_bytes=_vmem_limit(est)),
        name="moe_ln",
    )(x1, gate, grouped(w["wg"]), grouped(w["wu"]), grouped(w["wd"]), w["ln2_g"], w["ln2_b"])


def _pick_tile(n, pref):
    t = min(pref, n)
    while n % t:
        t //= 2
    return t


def _prepare_weights(w_in, mla_q_norm, mla_kv_norm, mla_w_uq, mla_w_ukv, w_o, ln1_g, ln1_b,
                     router_group_w, router_group_b, router_expert_w, router_expert_b,
                     expert_w_gate, expert_w_up, expert_w_down, ln2_g, ln2_b):
    depth = w_in.shape[0]
    w_perm = _gather_cols(w_in, _inproj_column_map()).astype(BF16)
    vec = lambda a: a.reshape(depth, 1, -1)
    rw = jnp.concatenate([jnp.moveaxis(router_expert_w, 1, 2).reshape(depth, D_MODEL, N_EXPERTS),
                          router_group_w,
                          jnp.zeros((depth, D_MODEL, LANES - N_EXPERTS - N_GROUPS), F32)], axis=-1)
    rb = jnp.concatenate([router_expert_b.reshape(depth, N_EXPERTS), router_group_b,
                          jnp.zeros((depth, LANES - N_EXPERTS - N_GROUPS), F32)], axis=-1)
    rwh = rw.astype(BF16)
    ex = lambda a: a.reshape((depth, N_EXPERTS) + a.shape[3:]).astype(BF16)
    return {
        "wa": w_perm[:, :, :_WA], "wb": w_perm[:, :, _WA:_WA + _WB], "wc": w_perm[:, :, _WA + _WB:],
        "wuq": _gather_cols(mla_w_uq, _uq_column_map()).astype(BF16),
        "wukv": _gather_cols(mla_w_ukv, _ukv_column_map()).astype(BF16),
        "qn": vec(mla_q_norm), "kvn": vec(mla_kv_norm),
        "wo": w_o.astype(BF16), "ln1_g": vec(ln1_g), "ln1_b": vec(ln1_b),
        "rwh": rwh, "rwl": (rw - rwh.astype(F32)).astype(BF16), "rb": vec(rb),
        "wg": ex(expert_w_gate), "wu": ex(expert_w_up), "wd": ex(expert_w_down),
        "ln2_g": vec(ln2_g), "ln2_b": vec(ln2_b),
    }


def kernel(x, positions, w_in, mla_q_norm, mla_kv_norm, mla_w_uq, mla_w_ukv, w_o, ln1_g, ln1_b,
           router_group_w, router_group_b, router_expert_w, router_expert_b,
           expert_w_gate, expert_w_up, expert_w_down, ln2_g, ln2_b):
    b, s, d_model = x.shape
    assert d_model == D_MODEL and w_in.shape[-1] == _IN_COLS
    t = b * s
    w = _prepare_weights(w_in, mla_q_norm, mla_kv_norm, mla_w_uq, mla_w_ukv, w_o, ln1_g, ln1_b,
                         router_group_w, router_group_b, router_expert_w, router_expert_b,
                         expert_w_gate, expert_w_up, expert_w_down, ln2_g, ln2_b)
    tabs = [a.reshape(t, LANES) for a in _rope_tables(positions)]
    tm_proj = _pick_tile(s, 512)
    tm_moe = _pick_tile(t, 1024)
    tq_mla = _pick_tile(s, 512)
    tq_dsa = _pick_tile(s, 256)
    n_cfg = len(DIL_CONFIGS)
    x2d = x.reshape(t, D_MODEL)
    for layer in range(w_in.shape[0]):
        (mq, mk, mv, dq, dk, dv, qi, ki, wi), dil_qkv = _inproj(layer, x2d, w, tabs, tm_proj, b, s)
        oa = _mla_attention(mq, mk, mv, b, s, tq_mla).reshape(t, -1)
        ob = _dsa_attention(dq, dk, dv, qi, ki, wi, b, s, tq_dsa).reshape(t, -1)
        dil = [_dilated_attention(dil_qkv[g], dil_qkv[n_cfg + g], dil_qkv[2 * n_cfg + g], g, b, s)
               for g in range(n_cfg)]
        x1, gate = _outproj(layer, x2d, oa, ob, dil, w, tm_proj, b, s)
        x2d = _moe(layer, x1, gate, w, tm_moe)
    return x2d.reshape(b, s, D_MODEL)
```

```python
import functools

import numpy as np
import jax
import jax.numpy as jnp
from jax import lax
from jax.experimental import pallas as pl
from jax.experimental.pallas import tpu as pltpu

D_MODEL = 1024
DEPTH = 4
HEAD_DIM = 64
MLA_HEADS = 8
MLA_Q_RANK = 256
MLA_KV_RANK = 128
MLA_NOPE = 64
MLA_ROPE = 32
MLA_V = 64
DSA_HEADS = 4
IDX_HEADS = 8
IDX_DIM = 32
DSA_TOPK_MAX = 256
DIL_CONFIGS = ((128, 1), (512, 4), (2048, 16))
DIL_HEADS = 4
ROPE_THETA = 500000.0
ROT_HEAD_DIMS = HEAD_DIM // 4
ROT_IDX_DIMS = IDX_DIM // 4
N_GROUPS = 4
EXPERTS_PER_GROUP = 8
N_EXPERTS = N_GROUPS * EXPERTS_PER_GROUP
EXPERT_HIDDEN = 256
DEEPNORM_ALPHA = (2.0 * DEPTH) ** 0.25
LN_EPS = 1e-5
RMS_EPS = 1e-6
NEG_INF = -1e30

LANES = 128
VMEM_CAP_BYTES = 60000 * 1024
INT_MIN = -(2 ** 31)

BF16 = jnp.bfloat16
F32 = jnp.float32

_OFF_CQ = 0
_OFF_CKV = _OFF_CQ + MLA_Q_RANK
_OFF_KPE = _OFF_CKV + MLA_KV_RANK
_OFF_BQ = _OFF_KPE + MLA_ROPE
_OFF_BK = _OFF_BQ + DSA_HEADS * HEAD_DIM
_OFF_BV = _OFF_BK + HEAD_DIM
_OFF_BQI = _OFF_BV + HEAD_DIM
_OFF_BKI = _OFF_BQI + IDX_HEADS * IDX_DIM
_OFF_BWI = _OFF_BKI + IDX_DIM
_OFF_C = _OFF_BWI + IDX_HEADS
_N_DIL = len(DIL_CONFIGS) * DIL_HEADS * HEAD_DIM
_IN_COLS = _OFF_C + 3 * _N_DIL

_WA = MLA_Q_RANK + MLA_KV_RANK + LANES
_WB = 5 * 256 + LANES
_WC = 3 * _N_DIL


def _vmem_limit(est_bytes):
    return int(min(max(2 * est_bytes, 32 * 1024 * 1024), VMEM_CAP_BYTES))


def _dot(a, b):
    return jnp.dot(a, b, preferred_element_type=F32)


def _dot_nt(a, b):
    return lax.dot_general(a, b, (((1,), (1,)), ((), ())), preferred_element_type=F32)


def _inproj_column_map():
    a = list(range(_OFF_CQ, _OFF_CQ + MLA_Q_RANK)) + list(range(_OFF_CKV, _OFF_CKV + MLA_KV_RANK))
    kpe = [-1] * LANES
    kpe[MLA_NOPE:MLA_NOPE + MLA_ROPE] = range(_OFF_KPE, _OFF_KPE + MLA_ROPE)
    a += kpe
    b = list(range(_OFF_BQ, _OFF_BQ + 256))
    b += list(range(_OFF_BK, _OFF_BK + HEAD_DIM)) * DSA_HEADS
    b += list(range(_OFF_BV, _OFF_BV + HEAD_DIM)) * DSA_HEADS
    b += list(range(_OFF_BQI, _OFF_BQI + 256))
    b += list(range(_OFF_BKI, _OFF_BKI + IDX_DIM)) * IDX_HEADS
    b += list(range(_OFF_BWI, _OFF_BWI + IDX_HEADS)) + [-1] * (LANES - IDX_HEADS)
    c = list(range(_OFF_C, _OFF_C + _WC))
    assert len(a) == _WA and len(b) == _WB and len(c) == _WC
    return np.asarray(a + b + c, np.int32)


def _gather_cols(w, cols):
    cols = np.asarray(cols, np.int32)
    g = jnp.take(w, jnp.asarray(np.maximum(cols, 0)), axis=-1)
    return jnp.where(jnp.asarray(cols >= 0), g, 0.0)


def _uq_column_map():
    cols = []
    for h in range(MLA_HEADS):
        base = h * (MLA_NOPE + MLA_ROPE)
        cols += list(range(base, base + MLA_NOPE + MLA_ROPE)) + [-1] * (LANES - MLA_NOPE - MLA_ROPE)
    return cols


def _ukv_column_map():
    kcols, vcols = [], []
    for h in range(MLA_HEADS):
        base = h * (MLA_NOPE + MLA_V)
        kcols += list(range(base, base + MLA_NOPE)) + [-1] * (LANES - MLA_NOPE)
        vcols += list(range(base + MLA_NOPE, base + MLA_NOPE + MLA_V)) + [-1] * (LANES - MLA_V)
    return kcols + vcols


def _rope_tables(positions):
    pos = positions.astype(F32)[..., None]

    def cs(rot_dims):
        inv = ROPE_THETA ** (-jnp.arange(0, rot_dims, 2, dtype=F32) / rot_dims)
        ang = pos * inv
        return jnp.cos(ang), jnp.sin(ang)

    def pattern(c, s, period, start):
        half = c.shape[-1]
        head = jnp.ones(c.shape[:-1] + (start,), F32)
        tail = jnp.ones(c.shape[:-1] + (period - start - 2 * half,), F32)
        cc = jnp.concatenate([head, c, c, tail], axis=-1)
        ss = jnp.concatenate([0.0 * head, -s, s, 0.0 * tail], axis=-1)
        reps = LANES // period
        return jnp.tile(cc, (1, 1, reps)), jnp.tile(ss, (1, 1, reps))

    cm, sm = pattern(*cs(MLA_ROPE), LANES, MLA_NOPE)
    ch, sh = pattern(*cs(ROT_HEAD_DIMS), HEAD_DIM, 0)
    ci, si = pattern(*cs(ROT_IDX_DIMS), IDX_DIM, 0)
    return cm, sm, ch, sh, ci, si


def _rope_block(u, c, s, first_half, half):
    rot = jnp.where(first_half, pltpu.roll(u, LANES - half, 1), pltpu.roll(u, half, 1))
    return u * c + rot * s


def _inproj_kernel(x_ref, wa_ref, wb_ref, wc_ref, wuq_ref, wukv_ref, qn_ref, kvn_ref,
                   cm_ref, sm_ref, ch_ref, sh_ref, ci_ref, si_ref,
                   mq_ref, mk_ref, mv_ref, dq_ref, dk_ref, dv_ref, qi_ref, ki_ref, wi_ref,
                   *dil_refs_and_scratch):
    dil_refs, perm_scr = dil_refs_and_scratch[:-1], dil_refs_and_scratch[-1]
    tm = x_ref.shape[0]
    lane = lax.broadcasted_iota(jnp.int32, (tm, LANES), 1)
    first_m = (lane >= MLA_NOPE) & (lane < MLA_NOPE + MLA_ROPE // 2)
    first_h = (lane & (HEAD_DIM - 1)) < ROT_HEAD_DIMS // 2
    first_i = (lane & (IDX_DIM - 1)) < ROT_IDX_DIMS // 2
    cm, sm = cm_ref[...], sm_ref[...]
    ch, sh = ch_ref[...], sh_ref[...]
    ci, si = ci_ref[...], si_ref[...]
    rope_m = functools.partial(_rope_block, c=cm, s=sm, first_half=first_m, half=MLA_ROPE // 2)
    rope_h = functools.partial(_rope_block, c=ch, s=sh, first_half=first_h, half=ROT_HEAD_DIMS // 2)
    rope_i = functools.partial(_rope_block, c=ci, s=si, first_half=first_i, half=ROT_IDX_DIMS // 2)

    xb = x_ref[...].astype(BF16)

    ua = _dot(xb, wa_ref[0])
    cq = ua[:, :MLA_Q_RANK]
    cqn = cq * lax.rsqrt(jnp.mean(cq * cq, axis=-1, keepdims=True) + RMS_EPS) * qn_ref[0]
    q = _dot(cqn.astype(BF16), wuq_ref[0])
    mla_scale = (MLA_NOPE + MLA_ROPE) ** -0.5
    for h in range(MLA_HEADS):
        sl = slice(h * LANES, (h + 1) * LANES)
        mq_ref[:, sl] = (rope_m(q[:, sl]) * mla_scale).astype(BF16)
    ckv = ua[:, MLA_Q_RANK:MLA_Q_RANK + MLA_KV_RANK]
    ckvn = ckv * lax.rsqrt(jnp.mean(ckv * ckv, axis=-1, keepdims=True) + RMS_EPS) * kvn_ref[0]
    kv = _dot(ckvn.astype(BF16), wukv_ref[0])
    kpe = rope_m(ua[:, MLA_Q_RANK + MLA_KV_RANK:])
    value_lanes = lane < HEAD_DIM
    for h in range(MLA_HEADS):
        sl = slice(h * LANES, (h + 1) * LANES)
        mk_ref[:, sl] = (kv[:, sl] + kpe).astype(BF16)
        v_blk = kv[:, (MLA_HEADS + h) * LANES:(MLA_HEADS + h + 1) * LANES]
        mv_ref[:, sl] = jnp.where(value_lanes, v_blk, 1.0).astype(BF16)

    ub = _dot(xb, wb_ref[0])
    dsa_scale = HEAD_DIM ** -0.5
    idx_scale = IDX_DIM ** -0.5 * IDX_HEADS ** -0.5
    for j in range(2):
        sl = slice(j * LANES, (j + 1) * LANES)
        dq_ref[:, sl] = (rope_h(ub[:, sl]) * dsa_scale).astype(BF16)
        dk_ref[:, sl] = rope_h(ub[:, 256 + j * LANES:256 + (j + 1) * LANES]).astype(BF16)
        qi_ref[:, sl] = rope_i(ub[:, 768 + j * LANES:768 + (j + 1) * LANES]).astype(BF16)
        ki_ref[:, sl] = rope_i(ub[:, 1024 + j * LANES:1024 + (j + 1) * LANES]).astype(BF16)
    dv_ref[...] = jnp.where(value_lanes, ub[:, 512:512 + LANES], 1.0).astype(BF16)
    wi_ref[...] = ub[:, 1280:] * idx_scale

    uc = _dot(xb, wc_ref[0])
    n_cfg = len(DIL_CONFIGS)
    for which in range(3):
        for g, (_, d) in enumerate(DIL_CONFIGS):
            halves = []
            for j in range(2):
                c0 = which * _N_DIL + g * 256 + j * LANES
                blk = uc[:, c0:c0 + LANES]
                if which < 2:
                    blk = rope_h(blk)
                if which == 0:
                    blk = blk * dsa_scale
                halves.append(blk)
            out_ref = dil_refs[which * n_cfg + g]
            if d == 1:
                out_ref[:, :LANES] = halves[0].astype(BF16)
                out_ref[:, LANES:] = halves[1].astype(BF16)
            else:
                for j in range(2):
                    slot = 2 * (which * (n_cfg - 1) + g - 1) + j
                    perm_scr[slot] = halves[j]
                    for r in range(d):
                        rows = perm_scr[slot, pl.ds(r, tm // d, stride=d), :]
                        out_ref[0, :, r * 256 + j * LANES:r * 256 + (j + 1) * LANES] = rows.astype(BF16)


def _inproj(layer, x2d, w, tabs, tm, b, s):
    t = x2d.shape[0]
    row = lambda n: pl.BlockSpec((tm, n), lambda i: (i, 0))
    wspec = lambda a: pl.BlockSpec((1,) + a.shape[1:], lambda i: (layer,) + (0,) * (a.ndim - 1))
    outs = [(8 * LANES, BF16), (8 * LANES, BF16), (8 * LANES, BF16),
            (256, BF16), (256, BF16), (LANES, BF16), (256, BF16), (256, BF16), (LANES, F32)]
    weights = [w["wa"], w["wb"], w["wc"], w["wuq"], w["wukv"], w["qn"], w["kvn"]]
    tiles_per_seq = s // tm
    dil_specs, dil_shapes = [], []
    for _ in range(3):
        for _, d in DIL_CONFIGS:
            if d == 1:
                dil_specs.append(row(256))
                dil_shapes.append(jax.ShapeDtypeStruct((t, 256), BF16))
            else:
                dil_specs.append(pl.BlockSpec((1, tm // d, d * 256),
                                              lambda i: (i // tiles_per_seq, i % tiles_per_seq, 0)))
                dil_shapes.append(jax.ShapeDtypeStruct((b, s // d, d * 256), BF16))
    est = (sum(int(np.prod(a.shape[1:])) * a.dtype.itemsize for a in weights) * 2
           + tm * D_MODEL * 4 * 2 + 6 * tm * LANES * 4 * 2
           + (sum(tm * n * jnp.dtype(d).itemsize for n, d in outs) + 9 * tm * 256 * 2) * 2
           + 6 * tm * 256 * 4 + tm * (_WA + _WB + _WC + 8 * LANES + 12 * LANES) * 4)
    res = pl.pallas_call(
        _inproj_kernel,
        grid=(t // tm,),
        in_specs=[row(D_MODEL)] + [wspec(a) for a in weights] + [row(LANES)] * 6,
        out_specs=[row(n) for n, _ in outs] + dil_specs,
        out_shape=[jax.ShapeDtypeStruct((t, n), d) for n, d in outs] + dil_shapes,
        scratch_shapes=[pltpu.VMEM((2 * 3 * (len(DIL_CONFIGS) - 1), tm, LANES), F32)],
        compiler_params=pltpu.CompilerParams(dimension_semantics=("arbitrary",),
                                             vmem_limit_bytes=_vmem_limit(est)),
        name="inproj",
    )(x2d, *weights, *tabs)
    return res[:len(outs)], res[len(outs):]


def _softmax_updates(scores, values, states):
    ps, partial = [], []
    for s, (m_prev, acc_prev) in zip(scores, states):
        m_new = jnp.maximum(m_prev, jnp.max(s, axis=-1, keepdims=True))
        ps.append(jnp.exp(s - m_new).astype(BF16))
        partial.append((m_new, jnp.exp(m_prev - m_new) * acc_prev))
    return tuple((m, acc + _dot(p, v)) for (m, acc), p, v in zip(partial, ps, values))


def _softmax_init(rows, n):
    return tuple((jnp.full((rows, 1), NEG_INF, F32), jnp.zeros((rows, LANES), F32)) for _ in range(n))


def _normalized_pair(state_even, state_odd):
    (_, acc_even), (_, acc_odd) = state_even, state_odd
    lane = lax.broadcasted_iota(jnp.int32, acc_even.shape, 1)
    o_even = acc_even / acc_even[:, HEAD_DIM:HEAD_DIM + 1]
    o_odd = acc_odd / acc_odd[:, HEAD_DIM:HEAD_DIM + 1]
    return jnp.where(lane < HEAD_DIM, o_even, pltpu.roll(o_odd, HEAD_DIM, 1))


def _mla_kernel(q_ref, k_ref, v_ref, o_ref, *, tq, nh):
    qi = pl.program_id(2)
    row = lax.broadcasted_iota(jnp.int32, (tq, tq), 0)
    col = lax.broadcasted_iota(jnp.int32, (tq, tq), 1)

    def step(j, states, diagonal):
        start = pl.multiple_of(j * tq, tq)
        scores, values = [], []
        for h in range(nh):
            hs = slice(h * LANES, (h + 1) * LANES)
            s = _dot_nt(q_ref[0, :, hs], k_ref[0, pl.ds(start, tq), hs])
            scores.append(jnp.where(col <= row, s, NEG_INF) if diagonal else s)
            values.append(v_ref[0, pl.ds(start, tq), hs])
        return _softmax_updates(scores, values, states)

    states = lax.fori_loop(0, qi, lambda j, st: step(j, st, False), _softmax_init(tq, nh))
    states = step(qi, states, True)
    for pair in range(nh // 2):
        o_ref[0, :, pair * LANES:(pair + 1) * LANES] = _normalized_pair(
            states[2 * pair], states[2 * pair + 1]).astype(BF16)


def _mla_attention(mq, mk, mv, b, s, tq, nh=4):
    est = (tq * nh * LANES * 2 * 2 + 2 * s * nh * LANES * 2 * 2 + tq * nh * MLA_V * 2 * 2
           + nh * tq * 3 * LANES * 4 + 4 * nh * tq * tq * 4)
    return pl.pallas_call(
        functools.partial(_mla_kernel, tq=tq, nh=nh),
        grid=(b, MLA_HEADS // nh, s // tq),
        in_specs=[pl.BlockSpec((1, tq, nh * LANES), lambda bi, h, i: (bi, i, h)),
                  pl.BlockSpec((1, s, nh * LANES), lambda bi, h, i: (bi, 0, h)),
                  pl.BlockSpec((1, s, nh * LANES), lambda bi, h, i: (bi, 0, h))],
        out_specs=pl.BlockSpec((1, tq, nh * MLA_V), lambda bi, h, i: (bi, i, h)),
        out_shape=jax.ShapeDtypeStruct((b, s, MLA_HEADS * MLA_V), BF16),
        compiler_params=pltpu.CompilerParams(dimension_semantics=("arbitrary",) * 3,
                                             vmem_limit_bytes=_vmem_limit(est)),
        name="mla_attn",
    )(mq.reshape(b, s, -1), mk.reshape(b, s, -1), mv.reshape(b, s, -1))


def _fold_lanes(x):
    acc = x[:, :LANES]
    for part in range(1, x.shape[1] // LANES):
        acc = acc + x[:, part * LANES:(part + 1) * LANES]
    return acc


def _dsa_kernel(q_ref, k_ref, v_ref, qi_ref, ki_ref, wi_ref, o_ref,
                key_scr, qim_scr, wb_scr, qm_scr, thr_scr, cnt_scr, *, tq, top_k):
    i = pl.program_id(1)
    n = i + 1
    lane = lax.broadcasted_iota(jnp.int32, (tq, 256), 1)
    row = lax.broadcasted_iota(jnp.int32, (tq, tq), 0)
    col = lax.broadcasted_iota(jnp.int32, (tq, tq), 1)

    def causal(c):
        return col <= row + (i - c) * tq

    qi_all = qi_ref[0]
    wi = wi_ref[0]
    for h in range(IDX_HEADS):
        qim_scr[h] = jnp.where((lane >> 5) == h, qi_all, jnp.zeros_like(qi_all))
        wb_scr[h] = jnp.broadcast_to(wi[:, h:h + 1], (tq, LANES))
    q_all = q_ref[0]
    for h in range(DSA_HEADS):
        qm_scr[h] = jnp.where((lane >> 6) == h, q_all, jnp.zeros_like(q_all))

    def score_chunk(c, carry):
        start = pl.multiple_of(c * tq, tq)
        ki_c = ki_ref[0, pl.ds(start, tq), :]
        sc = jnp.zeros((tq, tq), F32)
        for h in range(IDX_HEADS):
            wb = wb_scr[h]
            sc = sc + jnp.maximum(_dot_nt(qim_scr[h], ki_c), 0.0) * jnp.concatenate([wb] * (tq // LANES), axis=1)
        sc = jnp.where(causal(c), sc, NEG_INF) + 0.0
        bits = lax.bitcast_convert_type(sc, jnp.int32)
        key_scr[c] = bits ^ ((bits >> 31) & jnp.int32(0x7FFFFFFF))
        return carry

    lax.fori_loop(0, n, score_chunk, 0)

    kf = jnp.float32(top_k)

    def bisect_over(n_chunks):
        def count_ge(cand):
            acc = jnp.zeros((tq, LANES), F32)
            for c in range(n_chunks):
                acc = acc + _fold_lanes(jnp.where(key_scr[c] >= cand, 1.0, 0.0))
            return jnp.sum(acc, axis=-1, keepdims=True)

        t0 = jnp.where(count_ge(jnp.zeros((tq, 1), jnp.int32)) >= kf, jnp.int32(0), jnp.int32(INT_MIN))

        def bisect(b, t):
            cand = t | lax.shift_left(jnp.int32(1), 30 - b)
            return jnp.where(count_ge(cand) >= kf, cand, t)

        thr = lax.fori_loop(0, 31, bisect, t0, unroll=4)
        thr_scr[...] = thr
        cnt_scr[0] = count_ge(thr)
        cnt_scr[1] = count_ge(thr + 1)

    for n_chunks in range(1, key_scr.shape[0] + 1):
        pl.when(n == n_chunks)(functools.partial(bisect_over, n_chunks))
    thr = thr_scr[...]
    room = kf - cnt_scr[1]

    @pl.when(jnp.max(cnt_scr[0]) > kf)
    def _():
        upper = jnp.where(row < col, 1.0, 0.0).astype(BF16)

        def rank_chunk(c, before):
            key_c = key_scr[c]
            tie = key_c == thr
            tie_f = jnp.where(tie, 1.0, 0.0)
            rank = _dot(tie_f.astype(BF16), upper) + before
            key_scr[c] = jnp.where(tie & (rank >= room), jnp.int32(INT_MIN), key_c)
            return before + jnp.sum(tie_f, axis=-1, keepdims=True)

        lax.fori_loop(0, n, rank_chunk, jnp.zeros((tq, 1), F32))

    def attn_chunk(c, states):
        start = pl.multiple_of(c * tq, tq)
        k_c = k_ref[0, pl.ds(start, tq), :]
        v_c = v_ref[0, pl.ds(start, tq), :]
        ok = (key_scr[c] >= thr) & causal(c)
        scores = [jnp.where(ok, _dot_nt(qm_scr[h], k_c), NEG_INF) for h in range(DSA_HEADS)]
        return _softmax_updates(scores, [v_c] * DSA_HEADS, states)

    states = lax.fori_loop(0, n, attn_chunk, _softmax_init(tq, DSA_HEADS))
    for pair in range(DSA_HEADS // 2):
        o_ref[0, :, pair * LANES:(pair + 1) * LANES] = _normalized_pair(
            states[2 * pair], states[2 * pair + 1]).astype(BF16)


def _dsa_attention(dq, dk, dv, qi, ki, wi, b, s, tq):
    top_k = min(DSA_TOPK_MAX, s // 4)
    assert tq >= top_k and tq % LANES == 0
    qspec = lambda n: pl.BlockSpec((1, tq, n), lambda bi, i: (bi, i, 0))
    kspec = lambda n: pl.BlockSpec((1, s, n), lambda bi, i: (bi, 0, 0))
    nc = s // tq
    scratch = [pltpu.VMEM((nc, tq, tq), jnp.int32),
               pltpu.VMEM((IDX_HEADS, tq, 256), BF16), pltpu.VMEM((IDX_HEADS, tq, LANES), F32),
               pltpu.VMEM((DSA_HEADS, tq, 256), BF16), pltpu.VMEM((tq, 1), jnp.int32),
               pltpu.VMEM((2, tq, 1), F32)]
    est = (3 * tq * 256 * 2 * 2 + 3 * s * 256 * 2 * 2 + tq * LANES * 4 * 2 + tq * s * 4
           + 12 * tq * 256 * 2 + 8 * tq * LANES * 4 + 6 * tq * 256 * 4 + 16 * tq * tq * 4)
    r3 = lambda a: a.reshape(b, s, -1)
    return pl.pallas_call(
        functools.partial(_dsa_kernel, tq=tq, top_k=top_k),
        grid=(b, nc),
        in_specs=[qspec(256), kspec(256), kspec(LANES), qspec(256), kspec(256), qspec(LANES)],
        out_specs=qspec(256),
        out_shape=jax.ShapeDtypeStruct((b, s, 256), BF16),
        scratch_shapes=scratch,
        compiler_params=pltpu.CompilerParams(dimension_semantics=("arbitrary",) * 2,
                                             vmem_limit_bytes=_vmem_limit(est)),
        name="dsa_attn",
    )(r3(dq), r3(dk), r3(dv), r3(qi), r3(ki), r3(wi))


def _dilated_kernel(q_ref, kp_ref, kc_ref, vp_ref, vc_ref, o_ref, lse_ref, *, w, nrb, nres):
    j = pl.program_id(2)
    lane = lax.broadcasted_iota(jnp.int32, (w, 256), 1)
    row = lax.broadcasted_iota(jnp.int32, (w, 2 * w), 0)
    col = lax.broadcasted_iota(jnp.int32, (w, 2 * w), 1)
    rel = row + w - col
    in_window = (rel >= 0) & (rel <= w)
    first_rows_ok = in_window & (col + j * (2 * w) >= w)
    heads = [(lane >> 6) == h for h in range(DIL_HEADS)]
    units = []
    for rs in range(nres):
        cs = slice(rs * 256, (rs + 1) * 256)
        for rb in range(nrb):
            rows = slice(rb * w, (rb + 1) * w)
            q_all = q_ref[0, rows, cs]
            if rb == 0:
                k_prev, v_prev, ok = kp_ref[0, :, cs], vp_ref[0, :, cs], first_rows_ok
            else:
                prev_rows = slice((rb - 1) * w, rb * w)
                k_prev, v_prev, ok = kc_ref[0, prev_rows, cs], vc_ref[0, prev_rows, cs], in_window
            k = jnp.concatenate([k_prev, kc_ref[0, rows, cs]], axis=0)
            v = jnp.concatenate([v_prev, vc_ref[0, rows, cs]], axis=0)
            scores = [jnp.where(ok, _dot_nt(jnp.where(head, q_all, jnp.zeros_like(q_all)), k), NEG_INF)
                      for head in heads]
            units.append((rows, cs, v, scores))
    soft = []
    for rows, cs, v, scores in units:
        stats = []
        for sc in scores:
            m = jnp.max(sc, axis=-1, keepdims=True)
            p = jnp.exp(sc - m)
            stats.append((m, jnp.sum(p, axis=-1, keepdims=True), p.astype(BF16)))
        soft.append((rows, cs, v, stats))
    for rows, cs, v, stats in soft:
        out = jnp.zeros((w, 256), F32)
        lse = jnp.zeros((w, 256), F32)
        for head, (m, l, p) in zip(heads, stats):
            out = jnp.where(head, _dot(p, v) / l, out)
            lse = jnp.where(head, m + jnp.log(l), lse)
        o_ref[0, rows, cs] = out
        lse_ref[0, rows, cs] = lse


def _dilated_attention(q, k, v, g, b, s):
    window, d = DIL_CONFIGS[g]
    w = window // d
    steps = s // d
    assert steps % w == 0
    nrb = min(4, steps // w)
    nres = min(d, max(1, 4 // nrb))
    q, k, v = [a.reshape(b, steps, d * 256) for a in (q, k, v)]
    cur = pl.BlockSpec((1, nrb * w, nres * 256), lambda bi, c, j: (bi, j, c))
    prev = pl.BlockSpec((1, w, nres * 256), lambda bi, c, j: (bi, jnp.maximum(j * nrb - 1, 0), c))
    est = (3 * nrb * w * nres * 256 * 2 * 2 + 2 * w * nres * 256 * 2 * 2 + 2 * nrb * w * nres * 256 * 4 * 2
           + 16 * 8 * w * 2 * w * 4)
    return pl.pallas_call(
        functools.partial(_dilated_kernel, w=w, nrb=nrb, nres=nres),
        grid=(b, d // nres, steps // (nrb * w)),
        in_specs=[cur, prev, cur, prev, cur],
        out_specs=[cur, cur],
        out_shape=[jax.ShapeDtypeStruct((b, steps, d * 256), F32)] * 2,
        compiler_params=pltpu.CompilerParams(dimension_semantics=("arbitrary",) * 3,
                                             vmem_limit_bytes=_vmem_limit(est)),
        name=f"dilated_attn_{g}",
    )(q, k, k, v, v)


def _layer_norm(y, g, b):
    mu = jnp.mean(y, axis=-1, keepdims=True)
    yc = y - mu
    var = jnp.mean(yc * yc, axis=-1, keepdims=True)
    return yc * lax.rsqrt(var + LN_EPS) * g + b


def _outproj_kernel(x_ref, oa_ref, ob_ref, o0_ref, l0_ref, o1_ref, l1_ref, o2_ref, l2_ref,
                    wo_ref, g_ref, b_ref, rwh_ref, rwl_ref, rb_ref, x1_ref, gate_ref, tok_scr):
    tm = x_ref.shape[0]

    def token_major(ref, slot, d):
        for j in range(2):
            for r in range(d):
                c0 = r * 256 + j * LANES
                tok_scr[2 * slot + j, pl.ds(r, tm // d, stride=d), :] = ref[0, :, c0:c0 + LANES]
        return jnp.concatenate([tok_scr[2 * slot], tok_scr[2 * slot + 1]], axis=1)

    d1, d2 = DIL_CONFIGS[1][1], DIL_CONFIGS[2][1]
    o0, l0 = o0_ref[...], l0_ref[...]
    o1, l1 = token_major(o1_ref, 0, d1), token_major(l1_ref, 1, d1)
    o2, l2 = token_major(o2_ref, 2, d2), token_major(l2_ref, 3, d2)
    mx = jnp.maximum(jnp.maximum(l0, l1), l2)
    e0, e1, e2 = jnp.exp(l0 - mx), jnp.exp(l1 - mx), jnp.exp(l2 - mx)
    oc = (e0 * o0 + e1 * o1 + e2 * o2) / (e0 + e1 + e2)
    na = MLA_HEADS * MLA_V
    mix = (_dot(oa_ref[...], wo_ref[0, :na, :]) + _dot(ob_ref[...], wo_ref[0, na:na + 256, :])
           + _dot(oc.astype(BF16), wo_ref[0, na + 256:, :]))
    x1 = _layer_norm(DEEPNORM_ALPHA * x_ref[...] + mix, g_ref[0], b_ref[0])
    x1_ref[...] = x1

    xh = x1.astype(BF16)
    xl = (x1 - xh.astype(F32)).astype(BF16)
    logits = _dot(xh, rwh_ref[0]) + _dot(xl, rwh_ref[0]) + _dot(xh, rwl_ref[0]) + rb_ref[0]
    lane = lax.broadcasted_iota(jnp.int32, (tm, LANES), 1)
    big = jnp.int32(LANES)
    is_group = (lane >= N_EXPERTS) & (lane < N_EXPERTS + N_GROUPS)
    gl = jnp.where(is_group, logits, -jnp.inf)
    gmax = jnp.max(gl, axis=-1, keepdims=True)
    g_star = jnp.min(jnp.where(gl == gmax, lane - N_EXPERTS, big), axis=-1, keepdims=True)
    p_top = 1.0 / jnp.sum(jnp.exp(gl - gmax), axis=-1, keepdims=True)
    in_group = (lane < N_EXPERTS) & ((lane >> 3) == g_star)
    el = jnp.where(in_group, logits, -jnp.inf)
    v1 = jnp.max(el, axis=-1, keepdims=True)
    i1 = jnp.min(jnp.where(el == v1, lane, big), axis=-1, keepdims=True)
    el2 = jnp.where(lane == i1, -jnp.inf, el)
    v2 = jnp.max(el2, axis=-1, keepdims=True)
    i2 = jnp.min(jnp.where(el2 == v2, lane, big), axis=-1, keepdims=True)
    e21 = jnp.exp(v2 - v1)
    w1 = 1.0 / (1.0 + e21)
    w2 = e21 / (1.0 + e21)
    gate = p_top * (jnp.where(lane == i1, w1, 0.0) + jnp.where(lane == i2, w2, 0.0))
    gate_ref[...] = jnp.where(lane == N_EXPERTS, g_star.astype(F32), gate)


def _outproj(layer, x2d, oa, ob, dil, w, tm, b, s):
    t = x2d.shape[0]
    row = lambda n: pl.BlockSpec((tm, n), lambda i: (i, 0))
    wspec = lambda a: pl.BlockSpec((1,) + a.shape[1:], lambda i: (layer,) + (0,) * (a.ndim - 1))
    tiles_per_seq = s // tm
    res_major = lambda d: pl.BlockSpec((1, tm // d, d * 256), lambda i: (i // tiles_per_seq, i % tiles_per_seq, 0))
    d1, d2 = DIL_CONFIGS[1][1], DIL_CONFIGS[2][1]
    weights = [w["wo"], w["ln1_g"], w["ln1_b"], w["rwh"], w["rwl"], w["rb"]]
    est = (D_MODEL * D_MODEL * 2 * 2 + 2 * D_MODEL * LANES * 2 * 2 + tm * D_MODEL * 4 * 4
           + tm * (512 + 256) * 2 * 2 + 6 * tm * 256 * 4 * 2 + 4 * tm * 256 * 4 + tm * LANES * 4 * 2
           + 6 * tm * D_MODEL * 4)
    return pl.pallas_call(
        _outproj_kernel,
        grid=(t // tm,),
        in_specs=[row(D_MODEL), row(MLA_HEADS * MLA_V), row(256), row(256), row(256),
                  res_major(d1), res_major(d1), res_major(d2), res_major(d2)] + [wspec(a) for a in weights],
        out_specs=[row(D_MODEL), row(LANES)],
        out_shape=[jax.ShapeDtypeStruct((t, D_MODEL), F32), jax.ShapeDtypeStruct((t, LANES), F32)],
        scratch_shapes=[pltpu.VMEM((8, tm, LANES), F32)],
        compiler_params=pltpu.CompilerParams(dimension_semantics=("arbitrary",),
                                             vmem_limit_bytes=_vmem_limit(est)),
        name="outproj_ln_router",
    )(x2d, oa, ob, dil[0][0].reshape(t, 256), dil[0][1].reshape(t, 256),
      dil[1][0], dil[1][1], dil[2][0], dil[2][1], *weights)


def _moe_kernel(x_ref, gate_ref, wg_ref, wu_ref, wd_ref, g_ref, b_ref, o_ref,
                xb_scr, acc_scr, tri_scr, rank_scr, rank_t_scr, load_scr, gate_hi_scr, gate_lo_scr, *, sub):
    g = pl.program_id(1)
    wt = x_ref.shape[0]
    lane = lax.broadcasted_iota(jnp.int32, (wt, LANES), 1)

    @pl.when((pl.program_id(0) == 0) & (g == 0))
    def _():
        before = lax.broadcasted_iota(jnp.int32, (wt, wt), 1) < lax.broadcasted_iota(jnp.int32, (wt, wt), 0)
        tri_scr[...] = jnp.where(before, 1.0, 0.0).astype(BF16)

    @pl.when(g == 0)
    def _():
        xb_scr[...] = x_ref[...].astype(BF16)
        acc_scr[...] = jnp.zeros_like(acc_scr)
        gate = gate_ref[...]
        gate_hi = gate.astype(BF16)
        gate_hi_scr[...] = gate_hi
        gate_lo_scr[...] = (gate - gate_hi.astype(F32)).astype(BF16)
        g_star = jnp.sum(jnp.where(lane == N_EXPERTS, gate, 0.0), axis=-1, keepdims=True)
        onehot = jnp.where(lane.astype(F32) == g_star, 1.0, 0.0)
        rank = jnp.where(onehot > 0.0, _dot(tri_scr[...], onehot.astype(BF16)), -1.0)
        rank_scr[...] = rank
        rank_t_scr[...] = jnp.transpose(rank)
        load_scr[...] = jnp.broadcast_to(jnp.sum(onehot, axis=0, keepdims=True), load_scr.shape)

    group_lane = lane == g
    rank_col = jnp.sum(jnp.where(group_lane, rank_scr[...], 0.0), axis=-1, keepdims=True)
    rank_row = rank_t_scr[pl.ds(g, 1), :]
    load_lane = lax.broadcasted_iota(jnp.int32, load_scr.shape, 1)
    load = jnp.sum(jnp.where(load_lane == g, load_scr[...], 0.0)) * (1.0 / load_scr.shape[0])
    n_sub = (load.astype(jnp.int32) + (sub - 1)) // sub
    lane_sub = lax.broadcasted_iota(jnp.int32, (sub, LANES), 1)
    slot_rows = lax.broadcasted_iota(jnp.int32, (sub, wt), 0).astype(F32)
    slot_cols = lax.broadcasted_iota(jnp.int32, (wt, sub), 1).astype(F32)

    def sub_tile(j, carry):
        base = (j * sub).astype(F32)
        take = jnp.where(rank_row == slot_rows + base, 1.0, 0.0).astype(BF16)
        xs = _dot(take, xb_scr[...]).astype(BF16)
        gs = _dot(take, gate_hi_scr[...]) + _dot(take, gate_lo_scr[...])
        y = jnp.zeros((sub, D_MODEL), F32)
        for e in range(EXPERTS_PER_GROUP):
            a = _dot(xs, wg_ref[0, 0, e])
            u = _dot(xs, wu_ref[0, 0, e])
            ge = jnp.sum(jnp.where(lane_sub == g * EXPERTS_PER_GROUP + e, gs, 0.0), axis=-1, keepdims=True)
            h = (a / (1.0 + jnp.exp(-a))) * u * ge
            y = y + _dot(h.astype(BF16), wd_ref[0, 0, e])
        put = jnp.where(rank_col == slot_cols + base, 1.0, 0.0).astype(BF16)
        y_hi = y.astype(BF16)
        y_lo = (y - y_hi.astype(F32)).astype(BF16)
        acc_scr[...] += _dot(put, y_hi) + _dot(put, y_lo)
        return carry

    lax.fori_loop(0, n_sub, sub_tile, 0)

    @pl.when(g == pl.num_programs(1) - 1)
    def _():
        o_ref[...] = _layer_norm(DEEPNORM_ALPHA * x_ref[...] + acc_scr[...], g_ref[0], b_ref[0])


def _moe(layer, x1, gate, w, wt):
    t = x1.shape[0]
    f = EXPERT_HIDDEN
    epg = EXPERTS_PER_GROUP
    sub = min(wt, -(-(wt // N_GROUPS + wt // 32) // 16) * 16)
    est = (wt * D_MODEL * 4 * 4 + wt * LANES * 4 * 2 + 3 * epg * D_MODEL * f * 2 * 2
           + wt * D_MODEL * 6 + wt * wt * 2 + 6 * sub * D_MODEL * 4)
    wspec = lambda shape: pl.BlockSpec((1, 1) + shape, lambda i, g: (layer, g, 0, 0, 0))
    grouped = lambda a: a.reshape((a.shape[0], N_GROUPS, epg) + a.shape[2:])
    return pl.pallas_call(
        functools.partial(_moe_kernel, sub=sub),
        grid=(t // wt, N_GROUPS),
        in_specs=[pl.BlockSpec((wt, D_MODEL), lambda i, g: (i, 0)),
                  pl.BlockSpec((wt, LANES), lambda i, g: (i, 0)),
                  wspec((epg, D_MODEL, f)), wspec((epg, D_MODEL, f)), wspec((epg, f, D_MODEL)),
                  pl.BlockSpec((1, 1, D_MODEL), lambda i, g: (layer, 0, 0)),
                  pl.BlockSpec((1, 1, D_MODEL), lambda i, g: (layer, 0, 0))],
        out_specs=pl.BlockSpec((wt, D_MODEL), lambda i, g: (i, 0)),
        out_shape=jax.ShapeDtypeStruct((t, D_MODEL), F32),
        scratch_shapes=[pltpu.VMEM((wt, D_MODEL), BF16), pltpu.VMEM((wt, D_MODEL), F32),
                        pltpu.VMEM((wt, wt), BF16), pltpu.VMEM((wt, LANES), F32), pltpu.VMEM((LANES, wt), F32),
                        pltpu.VMEM((8, LANES), F32), pltpu.VMEM((wt, LANES), BF16), pltpu.VMEM((wt, LANES), BF16)],
        compiler_params=pltpu.CompilerParams(dimension_semantics=("arbitrary", "arbitrary"),
                                             vmem_limit_bytes=_vmem_limit(est)),
        name="moe_ln",
    )(x1, gate, grouped(w["wg"]), grouped(w["wu"]), grouped(w["wd"]), w["ln2_g"], w["ln2_b"])


def _pick_tile(n, pref):
    t = min(pref, n)
    while n % t:
        t //= 2
    return t


def _prepare_weights(w_in, mla_q_norm, mla_kv_norm, mla_w_uq, mla_w_ukv, w_o, ln1_g, ln1_b,
                     router_group_w, router_group_b, router_expert_w, router_expert_b,
                     expert_w_gate, expert_w_up, expert_w_down, ln2_g, ln2_b):
    depth = w_in.shape[0]
    w_perm = _gather_cols(w_in, _inproj_column_map()).astype(BF16)
    vec = lambda a: a.reshape(depth, 1, -1)
    rw = jnp.concatenate([jnp.moveaxis(router_expert_w, 1, 2).reshape(depth, D_MODEL, N_EXPERTS),
                          router_group_w,
                          jnp.zeros((depth, D_MODEL, LANES - N_EXPERTS - N_GROUPS), F32)], axis=-1)
    rb = jnp.concatenate([router_expert_b.reshape(depth, N_EXPERTS), router_group_b,
                          jnp.zeros((depth, LANES - N_EXPERTS - N_GROUPS), F32)], axis=-1)
    rwh = rw.astype(BF16)
    ex = lambda a: a.reshape((depth, N_EXPERTS) + a.shape[3:]).astype(BF16)
    return {
        "wa": w_perm[:, :, :_WA], "wb": w_perm[:, :, _WA:_WA + _WB], "wc": w_perm[:, :, _WA + _WB:],
        "wuq": _gather_cols(mla_w_uq, _uq_column_map()).astype(BF16),
        "wukv": _gather_cols(mla_w_ukv, _ukv_column_map()).astype(BF16),
        "qn": vec(mla_q_norm), "kvn": vec(mla_kv_norm),
        "wo": w_o.astype(BF16), "ln1_g": vec(ln1_g), "ln1_b": vec(ln1_b),
        "rwh": rwh, "rwl": (rw - rwh.astype(F32)).astype(BF16), "rb": vec(rb),
        "wg": ex(expert_w_gate), "wu": ex(expert_w_up), "wd": ex(expert_w_down),
        "ln2_g": vec(ln2_g), "ln2_b": vec(ln2_b),
    }


def kernel(x, positions, w_in, mla_q_norm, mla_kv_norm, mla_w_uq, mla_w_ukv, w_o, ln1_g, ln1_b,
           router_group_w, router_group_b, router_expert_w, router_expert_b,
           expert_w_gate, expert_w_up, expert_w_down, ln2_g, ln2_b):
    b, s, d_model = x.shape
    assert d_model == D_MODEL and w_in.shape[-1] == _IN_COLS
    t = b * s
    w = _prepare_weights(w_in, mla_q_norm, mla_kv_norm, mla_w_uq, mla_w_ukv, w_o, ln1_g, ln1_b,
                         router_group_w, router_group_b, router_expert_w, router_expert_b,
                         expert_w_gate, expert_w_up, expert_w_down, ln2_g, ln2_b)
    tabs = [a.reshape(t, LANES) for a in _rope_tables(positions)]
    tm_proj = _pick_tile(s, 512)
    tm_moe = _pick_tile(t, 1024)
    tq_mla = _pick_tile(s, 512)
    tq_dsa = _pick_tile(s, 256)
    n_cfg = len(DIL_CONFIGS)
    x2d = x.reshape(t, D_MODEL)
    for layer in range(w_in.shape[0]):
        (mq, mk, mv, dq, dk, dv, qi, ki, wi), dil_qkv = _inproj(layer, x2d, w, tabs, tm_proj, b, s)
        oa = _mla_attention(mq, mk, mv, b, s, tq_mla).reshape(t, -1)
        ob = _dsa_attention(dq, dk, dv, qi, ki, wi, b, s, tq_dsa).reshape(t, -1)
        dil = [_dilated_attention(dil_qkv[g], dil_qkv[n_cfg + g], dil_qkv[2 * n_cfg + g], g, b, s)
               for g in range(n_cfg)]
        x1, gate = _outproj(layer, x2d, oa, ob, dil, w, tm_proj, b, s)
        x2d = _moe(layer, x1, gate, w, tm_moe)
    return x2d.reshape(b, s, D_MODEL)
```

```python
import functools

import numpy as np
import jax
import jax.numpy as jnp
from jax import lax
from jax.experimental import pallas as pl
from jax.experimental.pallas import tpu as pltpu

D_MODEL = 1024
DEPTH = 4
HEAD_DIM = 64
MLA_HEADS = 8
MLA_Q_RANK = 256
MLA_KV_RANK = 128
MLA_NOPE = 64
MLA_ROPE = 32
MLA_V = 64
DSA_HEADS = 4
IDX_HEADS = 8
IDX_DIM = 32
DSA_TOPK_MAX = 256
DIL_CONFIGS = ((128, 1), (512, 4), (2048, 16))
DIL_HEADS = 4
ROPE_THETA = 500000.0
ROT_HEAD_DIMS = HEAD_DIM // 4
ROT_IDX_DIMS = IDX_DIM // 4
N_GROUPS = 4
EXPERTS_PER_GROUP = 8
N_EXPERTS = N_GROUPS * EXPERTS_PER_GROUP
EXPERT_HIDDEN = 256
DEEPNORM_ALPHA = (2.0 * DEPTH) ** 0.25
LN_EPS = 1e-5
RMS_EPS = 1e-6
NEG_INF = -1e30

LANES = 128
VMEM_CAP_BYTES = 60000 * 1024
INT_MIN = -(2 ** 31)

BF16 = jnp.bfloat16
F32 = jnp.float32

_OFF_CQ = 0
_OFF_CKV = _OFF_CQ + MLA_Q_RANK
_OFF_KPE = _OFF_CKV + MLA_KV_RANK
_OFF_BQ = _OFF_KPE + MLA_ROPE
_OFF_BK = _OFF_BQ + DSA_HEADS * HEAD_DIM
_OFF_BV = _OFF_BK + HEAD_DIM
_OFF_BQI = _OFF_BV + HEAD_DIM
_OFF_BKI = _OFF_BQI + IDX_HEADS * IDX_DIM
_OFF_BWI = _OFF_BKI + IDX_DIM
_OFF_C = _OFF_BWI + IDX_HEADS
_N_DIL = len(DIL_CONFIGS) * DIL_HEADS * HEAD_DIM
_IN_COLS = _OFF_C + 3 * _N_DIL

_WA = MLA_Q_RANK + MLA_KV_RANK + LANES
_WB = 5 * 256 + LANES
_WC = 3 * _N_DIL


def _vmem_limit(est_bytes):
    return int(min(max(2 * est_bytes, 32 * 1024 * 1024), VMEM_CAP_BYTES))


def _dot(a, b):
    return jnp.dot(a, b, preferred_element_type=F32)


def _dot_nt(a, b):
    return lax.dot_general(a, b, (((1,), (1,)), ((), ())), preferred_element_type=F32)


def _inproj_column_map():
    a = list(range(_OFF_CQ, _OFF_CQ + MLA_Q_RANK)) + list(range(_OFF_CKV, _OFF_CKV + MLA_KV_RANK))
    kpe = [-1] * LANES
    kpe[MLA_NOPE:MLA_NOPE + MLA_ROPE] = range(_OFF_KPE, _OFF_KPE + MLA_ROPE)
    a += kpe
    b = list(range(_OFF_BQ, _OFF_BQ + 256))
    b += list(range(_OFF_BK, _OFF_BK + HEAD_DIM)) * DSA_HEADS
    b += list(range(_OFF_BV, _OFF_BV + HEAD_DIM)) * DSA_HEADS
    b += list(range(_OFF_BQI, _OFF_BQI + 256))
    b += list(range(_OFF_BKI, _OFF_BKI + IDX_DIM)) * IDX_HEADS
    b += list(range(_OFF_BWI, _OFF_BWI + IDX_HEADS)) + [-1] * (LANES - IDX_HEADS)
    c = list(range(_OFF_C, _OFF_C + _WC))
    assert len(a) == _WA and len(b) == _WB and len(c) == _WC
    return np.asarray(a + b + c, np.int32)


def _gather_cols(w, cols):
    cols = np.asarray(cols, np.int32)
    g = jnp.take(w, jnp.asarray(np.maximum(cols, 0)), axis=-1)
    return jnp.where(jnp.asarray(cols >= 0), g, 0.0)


def _uq_column_map():
    cols = []
    for h in range(MLA_HEADS):
        base = h * (MLA_NOPE + MLA_ROPE)
        cols += list(range(base, base + MLA_NOPE + MLA_ROPE)) + [-1] * (LANES - MLA_NOPE - MLA_ROPE)
    return cols


def _ukv_column_map():
    kcols, vcols = [], []
    for h in range(MLA_HEADS):
        base = h * (MLA_NOPE + MLA_V)
        kcols += list(range(base, base + MLA_NOPE)) + [-1] * (LANES - MLA_NOPE)
        vcols += list(range(base + MLA_NOPE, base + MLA_NOPE + MLA_V)) + [-1] * (LANES - MLA_V)
    return kcols + vcols


def _rope_tables(positions):
    pos = positions.astype(F32)[..., None]

    def cs(rot_dims):
        inv = ROPE_THETA ** (-jnp.arange(0, rot_dims, 2, dtype=F32) / rot_dims)
        ang = pos * inv
        return jnp.cos(ang), jnp.sin(ang)

    def pattern(c, s, period, start):
        half = c.shape[-1]
        head = jnp.ones(c.shape[:-1] + (start,), F32)
        tail = jnp.ones(c.shape[:-1] + (period - start - 2 * half,), F32)
        cc = jnp.concatenate([head, c, c, tail], axis=-1)
        ss = jnp.concatenate([0.0 * head, -s, s, 0.0 * tail], axis=-1)
        reps = LANES // period
        return jnp.tile(cc, (1, 1, reps)), jnp.tile(ss, (1, 1, reps))

    cm, sm = pattern(*cs(MLA_ROPE), LANES, MLA_NOPE)
    ch, sh = pattern(*cs(ROT_HEAD_DIMS), HEAD_DIM, 0)
    ci, si = pattern(*cs(ROT_IDX_DIMS), IDX_DIM, 0)
    return cm, sm, ch, sh, ci, si


def _rope_block(u, c, s, first_half, half):
    rot = jnp.where(first_half, pltpu.roll(u, LANES - half, 1), pltpu.roll(u, half, 1))
    return u * c + rot * s


def _inproj_kernel(x_ref, wa_ref, wb_ref, wc_ref, wuq_ref, wukv_ref, qn_ref, kvn_ref,
                   cm_ref, sm_ref, ch_ref, sh_ref, ci_ref, si_ref,
                   mq_ref, mk_ref, mv_ref, dq_ref, dk_ref, dv_ref, qi_ref, ki_ref, wi_ref,
                   *dil_refs_and_scratch):
    dil_refs, perm_scr = dil_refs_and_scratch[:-1], dil_refs_and_scratch[-1]
    tm = x_ref.shape[0]
    lane = lax.broadcasted_iota(jnp.int32, (tm, LANES), 1)
    first_m = (lane >= MLA_NOPE) & (lane < MLA_NOPE + MLA_ROPE // 2)
    first_h = (lane & (HEAD_DIM - 1)) < ROT_HEAD_DIMS // 2
    first_i = (lane & (IDX_DIM - 1)) < ROT_IDX_DIMS // 2
    cm, sm = cm_ref[...], sm_ref[...]
    ch, sh = ch_ref[...], sh_ref[...]
    ci, si = ci_ref[...], si_ref[...]
    rope_m = functools.partial(_rope_block, c=cm, s=sm, first_half=first_m, half=MLA_ROPE // 2)
    rope_h = functools.partial(_rope_block, c=ch, s=sh, first_half=first_h, half=ROT_HEAD_DIMS // 2)
    rope_i = functools.partial(_rope_block, c=ci, s=si, first_half=first_i, half=ROT_IDX_DIMS // 2)

    xb = x_ref[...].astype(BF16)

    ua = _dot(xb, wa_ref[0])
    cq = ua[:, :MLA_Q_RANK]
    cqn = cq * lax.rsqrt(jnp.mean(cq * cq, axis=-1, keepdims=True) + RMS_EPS) * qn_ref[0]
    q = _dot(cqn.astype(BF16), wuq_ref[0])
    mla_scale = (MLA_NOPE + MLA_ROPE) ** -0.5
    for h in range(MLA_HEADS):
        sl = slice(h * LANES, (h + 1) * LANES)
        mq_ref[:, sl] = (rope_m(q[:, sl]) * mla_scale).astype(BF16)
    ckv = ua[:, MLA_Q_RANK:MLA_Q_RANK + MLA_KV_RANK]
    ckvn = ckv * lax.rsqrt(jnp.mean(ckv * ckv, axis=-1, keepdims=True) + RMS_EPS) * kvn_ref[0]
    kv = _dot(ckvn.astype(BF16), wukv_ref[0])
    kpe = rope_m(ua[:, MLA_Q_RANK + MLA_KV_RANK:])
    value_lanes = lane < HEAD_DIM
    for h in range(MLA_HEADS):
        sl = slice(h * LANES, (h + 1) * LANES)
        mk_ref[:, sl] = (kv[:, sl] + kpe).astype(BF16)
        v_blk = kv[:, (MLA_HEADS + h) * LANES:(MLA_HEADS + h + 1) * LANES]
        mv_ref[:, sl] = jnp.where(value_lanes, v_blk, 1.0).astype(BF16)

    ub = _dot(xb, wb_ref[0])
    dsa_scale = HEAD_DIM ** -0.5
    idx_scale = IDX_DIM ** -0.5 * IDX_HEADS ** -0.5
    for j in range(2):
        sl = slice(j * LANES, (j + 1) * LANES)
        dq_ref[:, sl] = (rope_h(ub[:, sl]) * dsa_scale).astype(BF16)
        dk_ref[:, sl] = rope_h(ub[:, 256 + j * LANES:256 + (j + 1) * LANES]).astype(BF16)
        qi_ref[:, sl] = rope_i(ub[:, 768 + j * LANES:768 + (j + 1) * LANES]).astype(BF16)
        ki_ref[:, sl] = rope_i(ub[:, 1024 + j * LANES:1024 + (j + 1) * LANES]).astype(BF16)
    dv_ref[...] = jnp.where(value_lanes, ub[:, 512:512 + LANES], 1.0).astype(BF16)
    wi_ref[...] = ub[:, 1280:] * idx_scale

    uc = _dot(xb, wc_ref[0])
    n_cfg = len(DIL_CONFIGS)
    for which in range(3):
        for g, (_, d) in enumerate(DIL_CONFIGS):
            halves = []
            for j in range(2):
                c0 = which * _N_DIL + g * 256 + j * LANES
                blk = uc[:, c0:c0 + LANES]
                if which < 2:
                    blk = rope_h(blk)
                if which == 0:
                    blk = blk * dsa_scale
                halves.append(blk)
            out_ref = dil_refs[which * n_cfg + g]
            if d == 1:
                out_ref[:, :LANES] = halves[0].astype(BF16)
                out_ref[:, LANES:] = halves[1].astype(BF16)
            else:
                for j in range(2):
                    slot = 2 * (which * (n_cfg - 1) + g - 1) + j
                    perm_scr[slot] = halves[j]
                    for r in range(d):
                        rows = perm_scr[slot, pl.ds(r, tm // d, stride=d), :]
                        out_ref[0, :, r * 256 + j * LANES:r * 256 + (j + 1) * LANES] = rows.astype(BF16)


def _inproj(layer, x2d, w, tabs, tm, b, s):
    t = x2d.shape[0]
    row = lambda n: pl.BlockSpec((tm, n), lambda i: (i, 0))
    wspec = lambda a: pl.BlockSpec((1,) + a.shape[1:], lambda i: (layer,) + (0,) * (a.ndim - 1))
    outs = [(8 * LANES, BF16), (8 * LANES, BF16), (8 * LANES, BF16),
            (256, BF16), (256, BF16), (LANES, BF16), (256, BF16), (256, BF16), (LANES, F32)]
    weights = [w["wa"], w["wb"], w["wc"], w["wuq"], w["wukv"], w["qn"], w["kvn"]]
    tiles_per_seq = s // tm
    dil_specs, dil_shapes = [], []
    for _ in range(3):
        for _, d in DIL_CONFIGS:
            if d == 1:
                dil_specs.append(row(256))
                dil_shapes.append(jax.ShapeDtypeStruct((t, 256), BF16))
            else:
                dil_specs.append(pl.BlockSpec((1, tm // d, d * 256),
                                              lambda i: (i // tiles_per_seq, i % tiles_per_seq, 0)))
                dil_shapes.append(jax.ShapeDtypeStruct((b, s // d, d * 256), BF16))
    est = (sum(int(np.prod(a.shape[1:])) * a.dtype.itemsize for a in weights) * 2
           + tm * D_MODEL * 4 * 2 + 6 * tm * LANES * 4 * 2
           + (sum(tm * n * jnp.dtype(d).itemsize for n, d in outs) + 9 * tm * 256 * 2) * 2
           + 6 * tm * 256 * 4 + tm * (_WA + _WB + _WC + 8 * LANES + 12 * LANES) * 4)
    res = pl.pallas_call(
        _inproj_kernel,
        grid=(t // tm,),
        in_specs=[row(D_MODEL)] + [wspec(a) for a in weights] + [row(LANES)] * 6,
        out_specs=[row(n) for n, _ in outs] + dil_specs,
        out_shape=[jax.ShapeDtypeStruct((t, n), d) for n, d in outs] + dil_shapes,
        scratch_shapes=[pltpu.VMEM((2 * 3 * (len(DIL_CONFIGS) - 1), tm, LANES), F32)],
        compiler_params=pltpu.CompilerParams(dimension_semantics=("arbitrary",),
                                             vmem_limit_bytes=_vmem_limit(est)),
        name="inproj",
    )(x2d, *weights, *tabs)
    return res[:len(outs)], res[len(outs):]


def _softmax_updates(scores, values, states):
    ps, partial = [], []
    for s, (m_prev, acc_prev) in zip(scores, states):
        m_new = jnp.maximum(m_prev, jnp.max(s, axis=-1, keepdims=True))
        ps.append(jnp.exp(s - m_new).astype(BF16))
        partial.append((m_new, jnp.exp(m_prev - m_new) * acc_prev))
    return tuple((m, acc + _dot(p, v)) for (m, acc), p, v in zip(partial, ps, values))


def _softmax_init(rows, n):
    return tuple((jnp.full((rows, 1), NEG_INF, F32), jnp.zeros((rows, LANES), F32)) for _ in range(n))


def _normalized_pair(state_even, state_odd):
    (_, acc_even), (_, acc_odd) = state_even, state_odd
    lane = lax.broadcasted_iota(jnp.int32, acc_even.shape, 1)
    o_even = acc_even / acc_even[:, HEAD_DIM:HEAD_DIM + 1]
    o_odd = acc_odd / acc_odd[:, HEAD_DIM:HEAD_DIM + 1]
    return jnp.where(lane < HEAD_DIM, o_even, pltpu.roll(o_odd, HEAD_DIM, 1))


def _mla_kernel(q_ref, k_ref, v_ref, o_ref, *, tq, nh):
    qi = pl.program_id(2)
    row = lax.broadcasted_iota(jnp.int32, (tq, tq), 0)
    col = lax.broadcasted_iota(jnp.int32, (tq, tq), 1)

    def step(j, states, diagonal):
        start = pl.multiple_of(j * tq, tq)
        scores, values = [], []
        for h in range(nh):
            hs = slice(h * LANES, (h + 1) * LANES)
            s = _dot_nt(q_ref[0, :, hs], k_ref[0, pl.ds(start, tq), hs])
            scores.append(jnp.where(col <= row, s, NEG_INF) if diagonal else s)
            values.append(v_ref[0, pl.ds(start, tq), hs])
        return _softmax_updates(scores, values, states)

    states = lax.fori_loop(0, qi, lambda j, st: step(j, st, False), _softmax_init(tq, nh))
    states = step(qi, states, True)
    for pair in range(nh // 2):
        o_ref[0, :, pair * LANES:(pair + 1) * LANES] = _normalized_pair(
            states[2 * pair], states[2 * pair + 1]).astype(BF16)


def _mla_attention(mq, mk, mv, b, s, tq, nh=4):
    est = (tq * nh * LANES * 2 * 2 + 2 * s * nh * LANES * 2 * 2 + tq * nh * MLA_V * 2 * 2
           + nh * tq * 3 * LANES * 4 + 4 * nh * tq * tq * 4)
    return pl.pallas_call(
        functools.partial(_mla_kernel, tq=tq, nh=nh),
        grid=(b, MLA_HEADS // nh, s // tq),
        in_specs=[pl.BlockSpec((1, tq, nh * LANES), lambda bi, h, i: (bi, i, h)),
                  pl.BlockSpec((1, s, nh * LANES), lambda bi, h, i: (bi, 0, h)),
                  pl.BlockSpec((1, s, nh * LANES), lambda bi, h, i: (bi, 0, h))],
        out_specs=pl.BlockSpec((1, tq, nh * MLA_V), lambda bi, h, i: (bi, i, h)),
        out_shape=jax.ShapeDtypeStruct((b, s, MLA_HEADS * MLA_V), BF16),
        compiler_params=pltpu.CompilerParams(dimension_semantics=("arbitrary",) * 3,
                                             vmem_limit_bytes=_vmem_limit(est)),
        name="mla_attn",
    )(mq.reshape(b, s, -1), mk.reshape(b, s, -1), mv.reshape(b, s, -1))


def _fold_lanes(x):
    acc = x[:, :LANES]
    for part in range(1, x.shape[1] // LANES):
        acc = acc + x[:, part * LANES:(part + 1) * LANES]
    return acc


def _dsa_kernel(q_ref, k_ref, v_ref, qi_ref, ki_ref, wi_ref, o_ref,
                key_scr, qim_scr, wb_scr, qm_scr, thr_scr, cnt_scr, *, tq, top_k):
    i = pl.program_id(1)
    n = i + 1
    lane = lax.broadcasted_iota(jnp.int32, (tq, 256), 1)
    row = lax.broadcasted_iota(jnp.int32, (tq, tq), 0)
    col = lax.broadcasted_iota(jnp.int32, (tq, tq), 1)

    def causal(c):
        return col <= row + (i - c) * tq

    qi_all = qi_ref[0]
    wi = wi_ref[0]
    for h in range(IDX_HEADS):
        qim_scr[h] = jnp.where((lane >> 5) == h, qi_all, jnp.zeros_like(qi_all))
        wb_scr[h] = jnp.broadcast_to(wi[:, h:h + 1], (tq, LANES))
    q_all = q_ref[0]
    for h in range(DSA_HEADS):
        qm_scr[h] = jnp.where((lane >> 6) == h, q_all, jnp.zeros_like(q_all))

    def score_chunk(c, carry):
        start = pl.multiple_of(c * tq, tq)
        ki_c = ki_ref[0, pl.ds(start, tq), :]
        sc = jnp.zeros((tq, tq), F32)
        for h in range(IDX_HEADS):
            wb = wb_scr[h]
            sc = sc + jnp.maximum(_dot_nt(qim_scr[h], ki_c), 0.0) * jnp.concatenate([wb] * (tq // LANES), axis=1)
        sc = jnp.where(causal(c), sc, NEG_INF) + 0.0
        bits = lax.bitcast_convert_type(sc, jnp.int32)
        key_scr[c] = bits ^ ((bits >> 31) & jnp.int32(0x7FFFFFFF))
        return carry

    lax.fori_loop(0, n, score_chunk, 0)

    kf = jnp.float32(top_k)

    def bisect_over(n_chunks):
        def count_ge(cand):
            acc = jnp.zeros((tq, LANES), F32)
            for c in range(n_chunks):
                acc = acc + _fold_lanes(jnp.where(key_scr[c] >= cand, 1.0, 0.0))
            return jnp.sum(acc, axis=-1, keepdims=True)

        t0 = jnp.where(count_ge(jnp.zeros((tq, 1), jnp.int32)) >= kf, jnp.int32(0), jnp.int32(INT_MIN))

        def bisect(b, t):
            cand = t | lax.shift_left(jnp.int32(1), 30 - b)
            return jnp.where(count_ge(cand) >= kf, cand, t)

        thr = lax.fori_loop(0, 31, bisect, t0, unroll=8)
        thr_scr[...] = thr
        cnt_scr[0] = count_ge(thr)
        cnt_scr[1] = count_ge(thr + 1)

    for n_chunks in range(1, key_scr.shape[0] + 1):
        pl.when(n == n_chunks)(functools.partial(bisect_over, n_chunks))
    thr = thr_scr[...]
    room = kf - cnt_scr[1]

    @pl.when(jnp.max(cnt_scr[0]) > kf)
    def _():
        upper = jnp.where(row < col, 1.0, 0.0).astype(BF16)

        def rank_chunk(c, before):
            key_c = key_scr[c]
            tie = key_c == thr
            tie_f = jnp.where(tie, 1.0, 0.0)
            rank = _dot(tie_f.astype(BF16), upper) + before
            key_scr[c] = jnp.where(tie & (rank >= room), jnp.int32(INT_MIN), key_c)
            return before + jnp.sum(tie_f, axis=-1, keepdims=True)

        lax.fori_loop(0, n, rank_chunk, jnp.zeros((tq, 1), F32))

    def attn_chunk(c, states):
        start = pl.multiple_of(c * tq, tq)
        k_c = k_ref[0, pl.ds(start, tq), :]
        v_c = v_ref[0, pl.ds(start, tq), :]
        ok = (key_scr[c] >= thr) & causal(c)
        scores = [jnp.where(ok, _dot_nt(qm_scr[h], k_c), NEG_INF) for h in range(DSA_HEADS)]
        return _softmax_updates(scores, [v_c] * DSA_HEADS, states)

    states = lax.fori_loop(0, n, attn_chunk, _softmax_init(tq, DSA_HEADS))
    for pair in range(DSA_HEADS // 2):
        o_ref[0, :, pair * LANES:(pair + 1) * LANES] = _normalized_pair(
            states[2 * pair], states[2 * pair + 1]).astype(BF16)


def _dsa_attention(dq, dk, dv, qi, ki, wi, b, s, tq):
    top_k = min(DSA_TOPK_MAX, s // 4)
    assert tq >= top_k and tq % LANES == 0
    qspec = lambda n: pl.BlockSpec((1, tq, n), lambda bi, i: (bi, i, 0))
    kspec = lambda n: pl.BlockSpec((1, s, n), lambda bi, i: (bi, 0, 0))
    nc = s // tq
    scratch = [pltpu.VMEM((nc, tq, tq), jnp.int32),
               pltpu.VMEM((IDX_HEADS, tq, 256), BF16), pltpu.VMEM((IDX_HEADS, tq, LANES), F32),
               pltpu.VMEM((DSA_HEADS, tq, 256), BF16), pltpu.VMEM((tq, 1), jnp.int32),
               pltpu.VMEM((2, tq, 1), F32)]
    est = (3 * tq * 256 * 2 * 2 + 3 * s * 256 * 2 * 2 + tq * LANES * 4 * 2 + tq * s * 4
           + 12 * tq * 256 * 2 + 8 * tq * LANES * 4 + 6 * tq * 256 * 4 + 16 * tq * tq * 4)
    r3 = lambda a: a.reshape(b, s, -1)
    return pl.pallas_call(
        functools.partial(_dsa_kernel, tq=tq, top_k=top_k),
        grid=(b, nc),
        in_specs=[qspec(256), kspec(256), kspec(LANES), qspec(256), kspec(256), qspec(LANES)],
        out_specs=qspec(256),
        out_shape=jax.ShapeDtypeStruct((b, s, 256), BF16),
        scratch_shapes=scratch,
        compiler_params=pltpu.CompilerParams(dimension_semantics=("arbitrary",) * 2,
                                             vmem_limit_bytes=_vmem_limit(est)),
        name="dsa_attn",
    )(r3(dq), r3(dk), r3(dv), r3(qi), r3(ki), r3(wi))


def _dilated_kernel(q_ref, kp_ref, kc_ref, vp_ref, vc_ref, o_ref, lse_ref, *, w, nrb, nres):
    j = pl.program_id(2)
    lane = lax.broadcasted_iota(jnp.int32, (w, 256), 1)
    row = lax.broadcasted_iota(jnp.int32, (w, 2 * w), 0)
    col = lax.broadcasted_iota(jnp.int32, (w, 2 * w), 1)
    rel = row + w - col
    in_window = (rel >= 0) & (rel <= w)
    first_rows_ok = in_window & (col + j * (2 * w) >= w)
    heads = [(lane >> 6) == h for h in range(DIL_HEADS)]
    units = []
    for rs in range(nres):
        cs = slice(rs * 256, (rs + 1) * 256)
        for rb in range(nrb):
            rows = slice(rb * w, (rb + 1) * w)
            q_all = q_ref[0, rows, cs]
            if rb == 0:
                k_prev, v_prev, ok = kp_ref[0, :, cs], vp_ref[0, :, cs], first_rows_ok
            else:
                prev_rows = slice((rb - 1) * w, rb * w)
                k_prev, v_prev, ok = kc_ref[0, prev_rows, cs], vc_ref[0, prev_rows, cs], in_window
            k = jnp.concatenate([k_prev, kc_ref[0, rows, cs]], axis=0)
            v = jnp.concatenate([v_prev, vc_ref[0, rows, cs]], axis=0)
            scores = [jnp.where(ok, _dot_nt(jnp.where(head, q_all, jnp.zeros_like(q_all)), k), NEG_INF)
                      for head in heads]
            units.append((rows, cs, v, scores))
    soft = []
    for rows, cs, v, scores in units:
        stats = []
        for sc in scores:
            m = jnp.max(sc, axis=-1, keepdims=True)
            p = jnp.exp(sc - m)
            stats.append((m, jnp.sum(p, axis=-1, keepdims=True), p.astype(BF16)))
        soft.append((rows, cs, v, stats))
    for rows, cs, v, stats in soft:
        out = jnp.zeros((w, 256), F32)
        lse = jnp.zeros((w, 256), F32)
        for head, (m, l, p) in zip(heads, stats):
            out = jnp.where(head, _dot(p, v) / l, out)
            lse = jnp.where(head, m + jnp.log(l), lse)
        o_ref[0, rows, cs] = out
        lse_ref[0, rows, cs] = lse


def _dilated_attention(q, k, v, g, b, s):
    window, d = DIL_CONFIGS[g]
    w = window // d
    steps = s // d
    assert steps % w == 0
    nrb = min(4, steps // w)
    nres = min(d, max(1, 4 // nrb))
    q, k, v = [a.reshape(b, steps, d * 256) for a in (q, k, v)]
    cur = pl.BlockSpec((1, nrb * w, nres * 256), lambda bi, c, j: (bi, j, c))
    prev = pl.BlockSpec((1, w, nres * 256), lambda bi, c, j: (bi, jnp.maximum(j * nrb - 1, 0), c))
    est = (3 * nrb * w * nres * 256 * 2 * 2 + 2 * w * nres * 256 * 2 * 2 + 2 * nrb * w * nres * 256 * 4 * 2
           + 16 * 8 * w * 2 * w * 4)
    return pl.pallas_call(
        functools.partial(_dilated_kernel, w=w, nrb=nrb, nres=nres),
        grid=(b, d // nres, steps // (nrb * w)),
        in_specs=[cur, prev, cur, prev, cur],
        out_specs=[cur, cur],
        out_shape=[jax.ShapeDtypeStruct((b, steps, d * 256), F32)] * 2,
        compiler_params=pltpu.CompilerParams(dimension_semantics=("arbitrary",) * 3,
                                             vmem_limit_bytes=_vmem_limit(est)),
        name=f"dilated_attn_{g}",
    )(q, k, k, v, v)


def _layer_norm(y, g, b):
    mu = jnp.mean(y, axis=-1, keepdims=True)
    yc = y - mu
    var = jnp.mean(yc * yc, axis=-1, keepdims=True)
    return yc * lax.rsqrt(var + LN_EPS) * g + b


def _outproj_kernel(x_ref, oa_ref, ob_ref, o0_ref, l0_ref, o1_ref, l1_ref, o2_ref, l2_ref,
                    wo_ref, g_ref, b_ref, rwh_ref, rwl_ref, rb_ref, x1_ref, gate_ref, tok_scr):
    tm = x_ref.shape[0]

    def token_major(ref, slot, d):
        for j in range(2):
            for r in range(d):
                c0 = r * 256 + j * LANES
                tok_scr[2 * slot + j, pl.ds(r, tm // d, stride=d), :] = ref[0, :, c0:c0 + LANES]
        return jnp.concatenate([tok_scr[2 * slot], tok_scr[2 * slot + 1]], axis=1)

    d1, d2 = DIL_CONFIGS[1][1], DIL_CONFIGS[2][1]
    o0, l0 = o0_ref[...], l0_ref[...]
    o1, l1 = token_major(o1_ref, 0, d1), token_major(l1_ref, 1, d1)
    o2, l2 = token_major(o2_ref, 2, d2), token_major(l2_ref, 3, d2)
    mx = jnp.maximum(jnp.maximum(l0, l1), l2)
    e0, e1, e2 = jnp.exp(l0 - mx), jnp.exp(l1 - mx), jnp.exp(l2 - mx)
    oc = (e0 * o0 + e1 * o1 + e2 * o2) / (e0 + e1 + e2)
    na = MLA_HEADS * MLA_V
    mix = (_dot(oa_ref[...], wo_ref[0, :na, :]) + _dot(ob_ref[...], wo_ref[0, na:na + 256, :])
           + _dot(oc.astype(BF16), wo_ref[0, na + 256:, :]))
    x1 = _layer_norm(DEEPNORM_ALPHA * x_ref[...] + mix, g_ref[0], b_ref[0])
    x1_ref[...] = x1

    xh = x1.astype(BF16)
    xl = (x1 - xh.astype(F32)).astype(BF16)
    logits = _dot(xh, rwh_ref[0]) + _dot(xl, rwh_ref[0]) + _dot(xh, rwl_ref[0]) + rb_ref[0]
    lane = lax.broadcasted_iota(jnp.int32, (tm, LANES), 1)
    big = jnp.int32(LANES)
    is_group = (lane >= N_EXPERTS) & (lane < N_EXPERTS + N_GROUPS)
    gl = jnp.where(is_group, logits, -jnp.inf)
    gmax = jnp.max(gl, axis=-1, keepdims=True)
    g_star = jnp.min(jnp.where(gl == gmax, lane - N_EXPERTS, big), axis=-1, keepdims=True)
    p_top = 1.0 / jnp.sum(jnp.exp(gl - gmax), axis=-1, keepdims=True)
    in_group = (lane < N_EXPERTS) & ((lane >> 3) == g_star)
    el = jnp.where(in_group, logits, -jnp.inf)
    v1 = jnp.max(el, axis=-1, keepdims=True)
    i1 = jnp.min(jnp.where(el == v1, lane, big), axis=-1, keepdims=True)
    el2 = jnp.where(lane == i1, -jnp.inf, el)
    v2 = jnp.max(el2, axis=-1, keepdims=True)
    i2 = jnp.min(jnp.where(el2 == v2, lane, big), axis=-1, keepdims=True)
    e21 = jnp.exp(v2 - v1)
    w1 = 1.0 / (1.0 + e21)
    w2 = e21 / (1.0 + e21)
    gate = p_top * (jnp.where(lane == i1, w1, 0.0) + jnp.where(lane == i2, w2, 0.0))
    gate_ref[...] = jnp.where(lane == N_EXPERTS, g_star.astype(F32), gate)


def _outproj(layer, x2d, oa, ob, dil, w, tm, b, s):
    t = x2d.shape[0]
    row = lambda n: pl.BlockSpec((tm, n), lambda i: (i, 0))
    wspec = lambda a: pl.BlockSpec((1,) + a.shape[1:], lambda i: (layer,) + (0,) * (a.ndim - 1))
    tiles_per_seq = s // tm
    res_major = lambda d: pl.BlockSpec((1, tm // d, d * 256), lambda i: (i // tiles_per_seq, i % tiles_per_seq, 0))
    d1, d2 = DIL_CONFIGS[1][1], DIL_CONFIGS[2][1]
    weights = [w["wo"], w["ln1_g"], w["ln1_b"], w["rwh"], w["rwl"], w["rb"]]
    est = (D_MODEL * D_MODEL * 2 * 2 + 2 * D_MODEL * LANES * 2 * 2 + tm * D_MODEL * 4 * 4
           + tm * (512 + 256) * 2 * 2 + 6 * tm * 256 * 4 * 2 + 4 * tm * 256 * 4 + tm * LANES * 4 * 2
           + 6 * tm * D_MODEL * 4)
    return pl.pallas_call(
        _outproj_kernel,
        grid=(t // tm,),
        in_specs=[row(D_MODEL), row(MLA_HEADS * MLA_V), row(256), row(256), row(256),
                  res_major(d1), res_major(d1), res_major(d2), res_major(d2)] + [wspec(a) for a in weights],
        out_specs=[row(D_MODEL), row(LANES)],
        out_shape=[jax.ShapeDtypeStruct((t, D_MODEL), F32), jax.ShapeDtypeStruct((t, LANES), F32)],
        scratch_shapes=[pltpu.VMEM((8, tm, LANES), F32)],
        compiler_params=pltpu.CompilerParams(dimension_semantics=("arbitrary",),
                                             vmem_limit_bytes=_vmem_limit(est)),
        name="outproj_ln_router",
    )(x2d, oa, ob, dil[0][0].reshape(t, 256), dil[0][1].reshape(t, 256),
      dil[1][0], dil[1][1], dil[2][0], dil[2][1], *weights)


def _moe_kernel(x_ref, gate_ref, wg_ref, wu_ref, wd_ref, g_ref, b_ref, o_ref,
                xb_scr, acc_scr, tri_scr, rank_scr, rank_t_scr, load_scr, gate_hi_scr, gate_lo_scr, *, sub):
    g = pl.program_id(1)
    wt = x_ref.shape[0]
    lane = lax.broadcasted_iota(jnp.int32, (wt, LANES), 1)

    @pl.when((pl.program_id(0) == 0) & (g == 0))
    def _():
        before = lax.broadcasted_iota(jnp.int32, (wt, wt), 1) < lax.broadcasted_iota(jnp.int32, (wt, wt), 0)
        tri_scr[...] = jnp.where(before, 1.0, 0.0).astype(BF16)

    @pl.when(g == 0)
    def _():
        xb_scr[...] = x_ref[...].astype(BF16)
        acc_scr[...] = jnp.zeros_like(acc_scr)
        gate = gate_ref[...]
        gate_hi = gate.astype(BF16)
        gate_hi_scr[...] = gate_hi
        gate_lo_scr[...] = (gate - gate_hi.astype(F32)).astype(BF16)
        g_star = jnp.sum(jnp.where(lane == N_EXPERTS, gate, 0.0), axis=-1, keepdims=True)
        onehot = jnp.where(lane.astype(F32) == g_star, 1.0, 0.0)
        rank = jnp.where(onehot > 0.0, _dot(tri_scr[...], onehot.astype(BF16)), -1.0)
        rank_scr[...] = rank
        rank_t_scr[...] = jnp.transpose(rank)
        load_scr[...] = jnp.broadcast_to(jnp.sum(onehot, axis=0, keepdims=True), load_scr.shape)

    group_lane = lane == g
    rank_col = jnp.sum(jnp.where(group_lane, rank_scr[...], 0.0), axis=-1, keepdims=True)
    rank_row = rank_t_scr[pl.ds(g, 1), :]
    load_lane = lax.broadcasted_iota(jnp.int32, load_scr.shape, 1)
    load = jnp.sum(jnp.where(load_lane == g, load_scr[...], 0.0)) * (1.0 / load_scr.shape[0])
    n_sub = (load.astype(jnp.int32) + (sub - 1)) // sub
    lane_sub = lax.broadcasted_iota(jnp.int32, (sub, LANES), 1)
    slot_rows = lax.broadcasted_iota(jnp.int32, (sub, wt), 0).astype(F32)
    slot_cols = lax.broadcasted_iota(jnp.int32, (wt, sub), 1).astype(F32)

    def sub_tile(j, carry):
        base = (j * sub).astype(F32)
        take = jnp.where(rank_row == slot_rows + base, 1.0, 0.0).astype(BF16)
        xs = _dot(take, xb_scr[...]).astype(BF16)
        gs = _dot(take, gate_hi_scr[...]) + _dot(take, gate_lo_scr[...])
        y = jnp.zeros((sub, D_MODEL), F32)
        for e in range(EXPERTS_PER_GROUP):
            a = _dot(xs, wg_ref[0, 0, e])
            u = _dot(xs, wu_ref[0, 0, e])
            ge = jnp.sum(jnp.where(lane_sub == g * EXPERTS_PER_GROUP + e, gs, 0.0), axis=-1, keepdims=True)
            h = (a / (1.0 + jnp.exp(-a))) * u * ge
            y = y + _dot(h.astype(BF16), wd_ref[0, 0, e])
        put = jnp.where(rank_col == slot_cols + base, 1.0, 0.0).astype(BF16)
        acc_scr[...] += _dot(put, y.astype(BF16))
        return carry

    lax.fori_loop(0, n_sub, sub_tile, 0)

    @pl.when(g == pl.num_programs(1) - 1)
    def _():
        o_ref[...] = _layer_norm(DEEPNORM_ALPHA * x_ref[...] + acc_scr[...], g_ref[0], b_ref[0])


def _moe(layer, x1, gate, w, wt):
    t = x1.shape[0]
    f = EXPERT_HIDDEN
    epg = EXPERTS_PER_GROUP
    sub = min(wt, -(-(wt // N_GROUPS + wt // 32) // 16) * 16)
    est = (wt * D_MODEL * 4 * 4 + wt * LANES * 4 * 2 + 3 * epg * D_MODEL * f * 2 * 2
           + wt * D_MODEL * 6 + wt * wt * 2 + 6 * sub * D_MODEL * 4)
    wspec = lambda shape: pl.BlockSpec((1, 1) + shape, lambda i, g: (layer, g, 0, 0, 0))
    grouped = lambda a: a.reshape((a.shape[0], N_GROUPS, epg) + a.shape[2:])
    return pl.pallas_call(
        functools.partial(_moe_kernel, sub=sub),
        grid=(t // wt, N_GROUPS),
        in_specs=[pl.BlockSpec((wt, D_MODEL), lambda i, g: (i, 0)),
                  pl.BlockSpec((wt, LANES), lambda i, g: (i, 0)),
                  wspec((epg, D_MODEL, f)), wspec((epg, D_MODEL, f)), wspec((epg, f, D_MODEL)),
                  pl.BlockSpec((1, 1, D_MODEL), lambda i, g: (layer, 0, 0)),
                  pl.BlockSpec((1, 1, D_MODEL), lambda i, g: (layer, 0, 0))],
        out_specs=pl.BlockSpec((wt, D_MODEL), lambda i, g: (i, 0)),
        out_shape=jax.ShapeDtypeStruct((t, D_MODEL), F32),
        scratch_shapes=[pltpu.VMEM((wt, D_MODEL), BF16), pltpu.VMEM((wt, D_MODEL), F32),
                        pltpu.VMEM((wt, wt), BF16), pltpu.VMEM((wt, LANES), F32), pltpu.VMEM((LANES, wt), F32),
                        pltpu.VMEM((8, LANES), F32), pltpu.VMEM((wt, LANES), BF16), pltpu.VMEM((wt, LANES), BF16)],
        compiler_params=pltpu.CompilerParams(dimension_semantics=("arbitrary", "arbitrary"),
                                             vmem_limit_bytes=_vmem_limit(est)),
        name="moe_ln",
    )(x1, gate, grouped(w["wg"]), grouped(w["wu"]), grouped(w["wd"]), w["ln2_g"], w["ln2_b"])


def _pick_tile(n, pref):
    t = min(pref, n)
    while n % t:
        t //= 2
    return t


def _prepare_weights(w_in, mla_q_norm, mla_kv_norm, mla_w_uq, mla_w_ukv, w_o, ln1_g, ln1_b,
                     router_group_w, router_group_b, router_expert_w, router_expert_b,
                     expert_w_gate, expert_w_up, expert_w_down, ln2_g, ln2_b):
    depth = w_in.shape[0]
    w_perm = _gather_cols(w_in, _inproj_column_map()).astype(BF16)
    vec = lambda a: a.reshape(depth, 1, -1)
    rw = jnp.concatenate([jnp.moveaxis(router_expert_w, 1, 2).reshape(depth, D_MODEL, N_EXPERTS),
                          router_group_w,
                          jnp.zeros((depth, D_MODEL, LANES - N_EXPERTS - N_GROUPS), F32)], axis=-1)
    rb = jnp.concatenate([router_expert_b.reshape(depth, N_EXPERTS), router_group_b,
                          jnp.zeros((depth, LANES - N_EXPERTS - N_GROUPS), F32)], axis=-1)
    rwh = rw.astype(BF16)
    ex = lambda a: a.reshape((depth, N_EXPERTS) + a.shape[3:]).astype(BF16)
    return {
        "wa": w_perm[:, :, :_WA], "wb": w_perm[:, :, _WA:_WA + _WB], "wc": w_perm[:, :, _WA + _WB:],
        "wuq": _gather_cols(mla_w_uq, _uq_column_map()).astype(BF16),
        "wukv": _gather_cols(mla_w_ukv, _ukv_column_map()).astype(BF16),
        "qn": vec(mla_q_norm), "kvn": vec(mla_kv_norm),
        "wo": w_o.astype(BF16), "ln1_g": vec(ln1_g), "ln1_b": vec(ln1_b),
        "rwh": rwh, "rwl": (rw - rwh.astype(F32)).astype(BF16), "rb": vec(rb),
        "wg": ex(expert_w_gate), "wu": ex(expert_w_up), "wd": ex(expert_w_down),
        "ln2_g": vec(ln2_g), "ln2_b": vec(ln2_b),
    }


def kernel(x, positions, w_in, mla_q_norm, mla_kv_norm, mla_w_uq, mla_w_ukv, w_o, ln1_g, ln1_b,
           router_group_w, router_group_b, router_expert_w, router_expert_b,
           expert_w_gate, expert_w_up, expert_w_down, ln2_g, ln2_b):
    b, s, d_model = x.shape
    assert d_model == D_MODEL and w_in.shape[-1] == _IN_COLS
    t = b * s
    w = _prepare_weights(w_in, mla_q_norm, mla_kv_norm, mla_w_uq, mla_w_ukv, w_o, ln1_g, ln1_b,
                         router_group_w, router_group_b, router_expert_w, router_expert_b,
                         expert_w_gate, expert_w_up, expert_w_down, ln2_g, ln2_b)
    tabs = [a.reshape(t, LANES) for a in _rope_tables(positions)]
    tm_proj = _pick_tile(s, 512)
    tm_moe = _pick_tile(t, 1024)
    tq_mla = _pick_tile(s, 512)
    tq_dsa = _pick_tile(s, 256)
    n_cfg = len(DIL_CONFIGS)
    x2d = x.reshape(t, D_MODEL)
    for layer in range(w_in.shape[0]):
        (mq, mk, mv, dq, dk, dv, qi, ki, wi), dil_qkv = _inproj(layer, x2d, w, tabs, tm_proj, b, s)
        oa = _mla_attention(mq, mk, mv, b, s, tq_mla).reshape(t, -1)
        ob = _dsa_attention(dq, dk, dv, qi, ki, wi, b, s, tq_dsa).reshape(t, -1)
        dil = [_dilated_attention(dil_qkv[g], dil_qkv[n_cfg + g], dil_qkv[2 * n_cfg + g], g, b, s)
               for g in range(n_cfg)]
        x1, gate = _outproj(layer, x2d, oa, ob, dil, w, tm_proj, b, s)
        x2d = _moe(layer, x1, gate, w, tm_moe)
    return x2d.reshape(b, s, D_MODEL)
```

```python
import functools

import numpy as np
import jax
import jax.numpy as jnp
from jax import lax
from jax.experimental import pallas as pl
from jax.experimental.pallas import tpu as pltpu

D_MODEL = 1024
DEPTH = 4
HEAD_DIM = 64
MLA_HEADS = 8
MLA_Q_RANK = 256
MLA_KV_RANK = 128
MLA_NOPE = 64
MLA_ROPE = 32
MLA_V = 64
DSA_HEADS = 4
IDX_HEADS = 8
IDX_DIM = 32
DSA_TOPK_MAX = 256
DIL_CONFIGS = ((128, 1), (512, 4), (2048, 16))
DIL_HEADS = 4
ROPE_THETA = 500000.0
ROT_HEAD_DIMS = HEAD_DIM // 4
ROT_IDX_DIMS = IDX_DIM // 4
N_GROUPS = 4
EXPERTS_PER_GROUP = 8
N_EXPERTS = N_GROUPS * EXPERTS_PER_GROUP
EXPERT_HIDDEN = 256
DEEPNORM_ALPHA = (2.0 * DEPTH) ** 0.25
LN_EPS = 1e-5
RMS_EPS = 1e-6
NEG_INF = -1e30

LANES = 128
VMEM_CAP_BYTES = 60000 * 1024
INT_MIN = -(2 ** 31)

BF16 = jnp.bfloat16
F32 = jnp.float32

_OFF_CQ = 0
_OFF_CKV = _OFF_CQ + MLA_Q_RANK
_OFF_KPE = _OFF_CKV + MLA_KV_RANK
_OFF_BQ = _OFF_KPE + MLA_ROPE
_OFF_BK = _OFF_BQ + DSA_HEADS * HEAD_DIM
_OFF_BV = _OFF_BK + HEAD_DIM
_OFF_BQI = _OFF_BV + HEAD_DIM
_OFF_BKI = _OFF_BQI + IDX_HEADS * IDX_DIM
_OFF_BWI = _OFF_BKI + IDX_DIM
_OFF_C = _OFF_BWI + IDX_HEADS
_N_DIL = len(DIL_CONFIGS) * DIL_HEADS * HEAD_DIM
_IN_COLS = _OFF_C + 3 * _N_DIL

_WA = MLA_Q_RANK + MLA_KV_RANK + LANES
_WB = 5 * 256 + LANES
_WC = 3 * _N_DIL


def _vmem_limit(est_bytes):
    return int(min(max(2 * est_bytes, 32 * 1024 * 1024), VMEM_CAP_BYTES))


def _dot(a, b):
    return jnp.dot(a, b, preferred_element_type=F32)


def _dot_nt(a, b):
    return lax.dot_general(a, b, (((1,), (1,)), ((), ())), preferred_element_type=F32)


def _inproj_column_map():
    a = list(range(_OFF_CQ, _OFF_CQ + MLA_Q_RANK)) + list(range(_OFF_CKV, _OFF_CKV + MLA_KV_RANK))
    kpe = [-1] * LANES
    kpe[MLA_NOPE:MLA_NOPE + MLA_ROPE] = range(_OFF_KPE, _OFF_KPE + MLA_ROPE)
    a += kpe
    b = list(range(_OFF_BQ, _OFF_BQ + 256))
    b += list(range(_OFF_BK, _OFF_BK + HEAD_DIM)) * DSA_HEADS
    b += list(range(_OFF_BV, _OFF_BV + HEAD_DIM)) * DSA_HEADS
    b += list(range(_OFF_BQI, _OFF_BQI + 256))
    b += list(range(_OFF_BKI, _OFF_BKI + IDX_DIM)) * IDX_HEADS
    b += list(range(_OFF_BWI, _OFF_BWI + IDX_HEADS)) + [-1] * (LANES - IDX_HEADS)
    c = list(range(_OFF_C, _OFF_C + _WC))
    assert len(a) == _WA and len(b) == _WB and len(c) == _WC
    return np.asarray(a + b + c, np.int32)


def _gather_cols(w, cols):
    cols = np.asarray(cols, np.int32)
    g = jnp.take(w, jnp.asarray(np.maximum(cols, 0)), axis=-1)
    return jnp.where(jnp.asarray(cols >= 0), g, 0.0)


def _uq_column_map():
    cols = []
    for h in range(MLA_HEADS):
        base = h * (MLA_NOPE + MLA_ROPE)
        cols += list(range(base, base + MLA_NOPE + MLA_ROPE)) + [-1] * (LANES - MLA_NOPE - MLA_ROPE)
    return cols


def _ukv_column_map():
    kcols, vcols = [], []
    for h in range(MLA_HEADS):
        base = h * (MLA_NOPE + MLA_V)
        kcols += list(range(base, base + MLA_NOPE)) + [-1] * (LANES - MLA_NOPE)
        vcols += list(range(base + MLA_NOPE, base + MLA_NOPE + MLA_V)) + [-1] * (LANES - MLA_V)
    return kcols + vcols


def _rope_tables(positions):
    pos = positions.astype(F32)[..., None]

    def cs(rot_dims):
        inv = ROPE_THETA ** (-jnp.arange(0, rot_dims, 2, dtype=F32) / rot_dims)
        ang = pos * inv
        return jnp.cos(ang), jnp.sin(ang)

    def pattern(c, s, period, start):
        half = c.shape[-1]
        head = jnp.ones(c.shape[:-1] + (start,), F32)
        tail = jnp.ones(c.shape[:-1] + (period - start - 2 * half,), F32)
        cc = jnp.concatenate([head, c, c, tail], axis=-1)
        ss = jnp.concatenate([0.0 * head, -s, s, 0.0 * tail], axis=-1)
        reps = LANES // period
        return jnp.tile(cc, (1, 1, reps)), jnp.tile(ss, (1, 1, reps))

    cm, sm = pattern(*cs(MLA_ROPE), LANES, MLA_NOPE)
    ch, sh = pattern(*cs(ROT_HEAD_DIMS), HEAD_DIM, 0)
    ci, si = pattern(*cs(ROT_IDX_DIMS), IDX_DIM, 0)
    return cm, sm, ch, sh, ci, si


def _rope_block(u, c, s, first_half, half):
    rot = jnp.where(first_half, pltpu.roll(u, LANES - half, 1), pltpu.roll(u, half, 1))
    return u * c + rot * s


def _inproj_kernel(x_ref, wa_ref, wb_ref, wc_ref, wuq_ref, wukv_ref, qn_ref, kvn_ref,
                   cm_ref, sm_ref, ch_ref, sh_ref, ci_ref, si_ref,
                   mq_ref, mk_ref, mv_ref, dq_ref, dk_ref, dv_ref, qi_ref, ki_ref, wi_ref,
                   *dil_refs_and_scratch):
    dil_refs, perm_scr = dil_refs_and_scratch[:-1], dil_refs_and_scratch[-1]
    tm = x_ref.shape[0]
    lane = lax.broadcasted_iota(jnp.int32, (tm, LANES), 1)
    first_m = (lane >= MLA_NOPE) & (lane < MLA_NOPE + MLA_ROPE // 2)
    first_h = (lane & (HEAD_DIM - 1)) < ROT_HEAD_DIMS // 2
    first_i = (lane & (IDX_DIM - 1)) < ROT_IDX_DIMS // 2
    cm, sm = cm_ref[...], sm_ref[...]
    ch, sh = ch_ref[...], sh_ref[...]
    ci, si = ci_ref[...], si_ref[...]
    rope_m = functools.partial(_rope_block, c=cm, s=sm, first_half=first_m, half=MLA_ROPE // 2)
    rope_h = functools.partial(_rope_block, c=ch, s=sh, first_half=first_h, half=ROT_HEAD_DIMS // 2)
    rope_i = functools.partial(_rope_block, c=ci, s=si, first_half=first_i, half=ROT_IDX_DIMS // 2)

    xb = x_ref[...].astype(BF16)

    ua = _dot(xb, wa_ref[0])
    cq = ua[:, :MLA_Q_RANK]
    cqn = cq * lax.rsqrt(jnp.mean(cq * cq, axis=-1, keepdims=True) + RMS_EPS) * qn_ref[0]
    q = _dot(cqn.astype(BF16), wuq_ref[0])
    mla_scale = (MLA_NOPE + MLA_ROPE) ** -0.5
    for h in range(MLA_HEADS):
        sl = slice(h * LANES, (h + 1) * LANES)
        mq_ref[:, sl] = (rope_m(q[:, sl]) * mla_scale).astype(BF16)
    ckv = ua[:, MLA_Q_RANK:MLA_Q_RANK + MLA_KV_RANK]
    ckvn = ckv * lax.rsqrt(jnp.mean(ckv * ckv, axis=-1, keepdims=True) + RMS_EPS) * kvn_ref[0]
    kv = _dot(ckvn.astype(BF16), wukv_ref[0])
    kpe = rope_m(ua[:, MLA_Q_RANK + MLA_KV_RANK:])
    value_lanes = lane < HEAD_DIM
    for h in range(MLA_HEADS):
        sl = slice(h * LANES, (h + 1) * LANES)
        mk_ref[:, sl] = (kv[:, sl] + kpe).astype(BF16)
        v_blk = kv[:, (MLA_HEADS + h) * LANES:(MLA_HEADS + h + 1) * LANES]
        mv_ref[:, sl] = jnp.where(value_lanes, v_blk, 1.0).astype(BF16)

    ub = _dot(xb, wb_ref[0])
    dsa_scale = HEAD_DIM ** -0.5
    idx_scale = IDX_DIM ** -0.5 * IDX_HEADS ** -0.5
    for j in range(2):
        sl = slice(j * LANES, (j + 1) * LANES)
        dq_ref[:, sl] = (rope_h(ub[:, sl]) * dsa_scale).astype(BF16)
        dk_ref[:, sl] = rope_h(ub[:, 256 + j * LANES:256 + (j + 1) * LANES]).astype(BF16)
        qi_ref[:, sl] = rope_i(ub[:, 768 + j * LANES:768 + (j + 1) * LANES]).astype(BF16)
        ki_ref[:, sl] = rope_i(ub[:, 1024 + j * LANES:1024 + (j + 1) * LANES]).astype(BF16)
    dv_ref[...] = jnp.where(value_lanes, ub[:, 512:512 + LANES], 1.0).astype(BF16)
    wi_ref[...] = ub[:, 1280:] * idx_scale

    uc = _dot(xb, wc_ref[0])
    n_cfg = len(DIL_CONFIGS)
    for which in range(3):
        for g, (_, d) in enumerate(DIL_CONFIGS):
            halves = []
            for j in range(2):
                c0 = which * _N_DIL + g * 256 + j * LANES
                blk = uc[:, c0:c0 + LANES]
                if which < 2:
                    blk = rope_h(blk)
                if which == 0:
                    blk = blk * dsa_scale
                halves.append(blk)
            out_ref = dil_refs[which * n_cfg + g]
            if d == 1:
                out_ref[:, :LANES] = halves[0].astype(BF16)
                out_ref[:, LANES:] = halves[1].astype(BF16)
            else:
                for j in range(2):
                    slot = 2 * (which * (n_cfg - 1) + g - 1) + j
                    perm_scr[slot] = halves[j]
                    for r in range(d):
                        rows = perm_scr[slot, pl.ds(r, tm // d, stride=d), :]
                        out_ref[0, :, r * 256 + j * LANES:r * 256 + (j + 1) * LANES] = rows.astype(BF16)


def _inproj(layer, x2d, w, tabs, tm, b, s):
    t = x2d.shape[0]
    row = lambda n: pl.BlockSpec((tm, n), lambda i: (i, 0))
    wspec = lambda a: pl.BlockSpec((1,) + a.shape[1:], lambda i: (layer,) + (0,) * (a.ndim - 1))
    outs = [(8 * LANES, BF16), (8 * LANES, BF16), (8 * LANES, BF16),
            (256, BF16), (256, BF16), (LANES, BF16), (256, BF16), (256, BF16), (LANES, F32)]
    weights = [w["wa"], w["wb"], w["wc"], w["wuq"], w["wukv"], w["qn"], w["kvn"]]
    tiles_per_seq = s // tm
    dil_specs, dil_shapes = [], []
    for _ in range(3):
        for _, d in DIL_CONFIGS:
            if d == 1:
                dil_specs.append(row(256))
                dil_shapes.append(jax.ShapeDtypeStruct((t, 256), BF16))
            else:
                dil_specs.append(pl.BlockSpec((1, tm // d, d * 256),
                                              lambda i: (i // tiles_per_seq, i % tiles_per_seq, 0)))
                dil_shapes.append(jax.ShapeDtypeStruct((b, s // d, d * 256), BF16))
    est = (sum(int(np.prod(a.shape[1:])) * a.dtype.itemsize for a in weights) * 2
           + tm * D_MODEL * 4 * 2 + 6 * tm * LANES * 4 * 2
           + (sum(tm * n * jnp.dtype(d).itemsize for n, d in outs) + 9 * tm * 256 * 2) * 2
           + 6 * tm * 256 * 4 + tm * (_WA + _WB + _WC + 8 * LANES + 12 * LANES) * 4)
    res = pl.pallas_call(
        _inproj_kernel,
        grid=(t // tm,),
        in_specs=[row(D_MODEL)] + [wspec(a) for a in weights] + [row(LANES)] * 6,
        out_specs=[row(n) for n, _ in outs] + dil_specs,
        out_shape=[jax.ShapeDtypeStruct((t, n), d) for n, d in outs] + dil_shapes,
        scratch_shapes=[pltpu.VMEM((2 * 3 * (len(DIL_CONFIGS) - 1), tm, LANES), F32)],
        compiler_params=pltpu.CompilerParams(dimension_semantics=("arbitrary",),
                                             vmem_limit_bytes=_vmem_limit(est)),
        name="inproj",
    )(x2d, *weights, *tabs)
    return res[:len(outs)], res[len(outs):]


def _softmax_updates(scores, values, states):
    ps, partial = [], []
    for s, (m_prev, acc_prev) in zip(scores, states):
        m_new = jnp.maximum(m_prev, jnp.max(s, axis=-1, keepdims=True))
        ps.append(jnp.exp(s - m_new).astype(BF16))
        partial.append((m_new, jnp.exp(m_prev - m_new) * acc_prev))
    return tuple((m, acc + _dot(p, v)) for (m, acc), p, v in zip(partial, ps, values))


def _softmax_init(rows, n):
    return tuple((jnp.full((rows, 1), NEG_INF, F32), jnp.zeros((rows, LANES), F32)) for _ in range(n))


def _normalized_pair(state_even, state_odd):
    (_, acc_even), (_, acc_odd) = state_even, state_odd
    lane = lax.broadcasted_iota(jnp.int32, acc_even.shape, 1)
    o_even = acc_even / acc_even[:, HEAD_DIM:HEAD_DIM + 1]
    o_odd = acc_odd / acc_odd[:, HEAD_DIM:HEAD_DIM + 1]
    return jnp.where(lane < HEAD_DIM, o_even, pltpu.roll(o_odd, HEAD_DIM, 1))


def _mla_kernel(q_ref, k_ref, v_ref, o_ref, *, tq, nh):
    qi = pl.program_id(2)
    row = lax.broadcasted_iota(jnp.int32, (tq, tq), 0)
    col = lax.broadcasted_iota(jnp.int32, (tq, tq), 1)

    def step(j, states, diagonal):
        start = pl.multiple_of(j * tq, tq)
        scores, values = [], []
        for h in range(nh):
            hs = slice(h * LANES, (h + 1) * LANES)
            s = _dot_nt(q_ref[0, :, hs], k_ref[0, pl.ds(start, tq), hs])
            scores.append(jnp.where(col <= row, s, NEG_INF) if diagonal else s)
            values.append(v_ref[0, pl.ds(start, tq), hs])
        return _softmax_updates(scores, values, states)

    states = lax.fori_loop(0, qi, lambda j, st: step(j, st, False), _softmax_init(tq, nh))
    states = step(qi, states, True)
    for pair in range(nh // 2):
        o_ref[0, :, pair * LANES:(pair + 1) * LANES] = _normalized_pair(
            states[2 * pair], states[2 * pair + 1]).astype(BF16)


def _mla_attention(mq, mk, mv, b, s, tq, nh=4):
    est = (tq * nh * LANES * 2 * 2 + 2 * s * nh * LANES * 2 * 2 + tq * nh * MLA_V * 2 * 2
           + nh * tq * 3 * LANES * 4 + 4 * nh * tq * tq * 4)
    return pl.pallas_call(
        functools.partial(_mla_kernel, tq=tq, nh=nh),
        grid=(b, MLA_HEADS // nh, s // tq),
        in_specs=[pl.BlockSpec((1, tq, nh * LANES), lambda bi, h, i: (bi, i, h)),
                  pl.BlockSpec((1, s, nh * LANES), lambda bi, h, i: (bi, 0, h)),
                  pl.BlockSpec((1, s, nh * LANES), lambda bi, h, i: (bi, 0, h))],
        out_specs=pl.BlockSpec((1, tq, nh * MLA_V), lambda bi, h, i: (bi, i, h)),
        out_shape=jax.ShapeDtypeStruct((b, s, MLA_HEADS * MLA_V), BF16),
        compiler_params=pltpu.CompilerParams(dimension_semantics=("arbitrary",) * 3,
                                             vmem_limit_bytes=_vmem_limit(est)),
        name="mla_attn",
    )(mq.reshape(b, s, -1), mk.reshape(b, s, -1), mv.reshape(b, s, -1))


def _fold_lanes(x):
    acc = x[:, :LANES]
    for part in range(1, x.shape[1] // LANES):
        acc = acc + x[:, part * LANES:(part + 1) * LANES]
    return acc


def _dsa_kernel(q_ref, k_ref, v_ref, qi_ref, ki_ref, wi_ref, o_ref,
                key_scr, qim_scr, wb_scr, qm_scr, thr_scr, cnt_scr, *, tq, top_k):
    i = pl.program_id(1)
    n = i + 1
    lane = lax.broadcasted_iota(jnp.int32, (tq, 256), 1)
    row = lax.broadcasted_iota(jnp.int32, (tq, tq), 0)
    col = lax.broadcasted_iota(jnp.int32, (tq, tq), 1)

    def causal(c):
        return col <= row + (i - c) * tq

    qi_all = qi_ref[0]
    wi = wi_ref[0]
    for h in range(IDX_HEADS):
        qim_scr[h] = jnp.where((lane >> 5) == h, qi_all, jnp.zeros_like(qi_all))
        wb_scr[h] = jnp.broadcast_to(wi[:, h:h + 1], (tq, LANES))
    q_all = q_ref[0]
    for h in range(DSA_HEADS):
        qm_scr[h] = jnp.where((lane >> 6) == h, q_all, jnp.zeros_like(q_all))

    def score_chunk(c, carry):
        start = pl.multiple_of(c * tq, tq)
        ki_c = ki_ref[0, pl.ds(start, tq), :]
        sc = jnp.zeros((tq, tq), F32)
        for h in range(IDX_HEADS):
            wb = wb_scr[h]
            sc = sc + jnp.maximum(_dot_nt(qim_scr[h], ki_c), 0.0) * jnp.concatenate([wb] * (tq // LANES), axis=1)
        sc = jnp.where(causal(c), sc, NEG_INF) + 0.0
        bits = lax.bitcast_convert_type(sc, jnp.int32)
        key_scr[c] = bits ^ ((bits >> 31) & jnp.int32(0x7FFFFFFF))
        return carry

    lax.fori_loop(0, n, score_chunk, 0)

    kf = jnp.float32(top_k)

    def bisect_over(n_chunks):
        if n_chunks * tq <= top_k:
            thr_scr[...] = jnp.full((tq, 1), INT_MIN, jnp.int32)
            cnt_scr[...] = jnp.full(cnt_scr.shape, n_chunks * tq, F32)
            return

        def count_ge(cand):
            acc = jnp.zeros((tq, LANES), F32)
            for c in range(n_chunks):
                acc = acc + _fold_lanes(jnp.where(key_scr[c] >= cand, 1.0, 0.0))
            return jnp.sum(acc, axis=-1, keepdims=True)

        t0 = jnp.where(count_ge(jnp.zeros((tq, 1), jnp.int32)) >= kf, jnp.int32(0), jnp.int32(INT_MIN))

        def bisect(b, t):
            cand = t | lax.shift_left(jnp.int32(1), 30 - b)
            return jnp.where(count_ge(cand) >= kf, cand, t)

        thr = lax.fori_loop(0, 31, bisect, t0, unroll=8)
        thr_scr[...] = thr
        cnt_scr[0] = count_ge(thr)
        cnt_scr[1] = count_ge(thr + 1)

    for n_chunks in range(1, key_scr.shape[0] + 1):
        pl.when(n == n_chunks)(functools.partial(bisect_over, n_chunks))
    thr = thr_scr[...]
    room = kf - cnt_scr[1]

    @pl.when(jnp.max(cnt_scr[0]) > kf)
    def _():
        upper = jnp.where(row < col, 1.0, 0.0).astype(BF16)

        def rank_chunk(c, before):
            key_c = key_scr[c]
            tie = key_c == thr
            tie_f = jnp.where(tie, 1.0, 0.0)
            rank = _dot(tie_f.astype(BF16), upper) + before
            key_scr[c] = jnp.where(tie & (rank >= room), jnp.int32(INT_MIN), key_c)
            return before + jnp.sum(tie_f, axis=-1, keepdims=True)

        lax.fori_loop(0, n, rank_chunk, jnp.zeros((tq, 1), F32))

    def attn_block(c0, width, states):
        start = pl.multiple_of(c0 * tq, tq)
        k_c = k_ref[0, pl.ds(start, width * tq), :]
        v_c = v_ref[0, pl.ds(start, width * tq), :]
        ok = jnp.concatenate([(key_scr[c0 + w] >= thr) & causal(c0 + w) for w in range(width)], axis=1)
        scores = [jnp.where(ok, _dot_nt(qm_scr[h], k_c), NEG_INF) for h in range(DSA_HEADS)]
        return _softmax_updates(scores, [v_c] * DSA_HEADS, states)

    states = lax.fori_loop(0, lax.shift_right_logical(n, 2), lambda p, st: attn_block(4 * p, 4, st),
                           _softmax_init(tq, DSA_HEADS))
    states = lax.fori_loop(0, lax.shift_right_logical(n, 1) & 1, lambda _, st: attn_block(n & ~3, 2, st), states)
    states = lax.fori_loop(0, n & 1, lambda _, st: attn_block(n - 1, 1, st), states)
    for pair in range(DSA_HEADS // 2):
        o_ref[0, :, pair * LANES:(pair + 1) * LANES] = _normalized_pair(
            states[2 * pair], states[2 * pair + 1]).astype(BF16)


def _dsa_attention(dq, dk, dv, qi, ki, wi, b, s, tq):
    top_k = min(DSA_TOPK_MAX, s // 4)
    assert tq >= top_k and tq % LANES == 0
    qspec = lambda n: pl.BlockSpec((1, tq, n), lambda bi, i: (bi, i, 0))
    kspec = lambda n: pl.BlockSpec((1, s, n), lambda bi, i: (bi, 0, 0))
    nc = s // tq
    scratch = [pltpu.VMEM((nc, tq, tq), jnp.int32),
               pltpu.VMEM((IDX_HEADS, tq, 256), BF16), pltpu.VMEM((IDX_HEADS, tq, LANES), F32),
               pltpu.VMEM((DSA_HEADS, tq, 256), BF16), pltpu.VMEM((tq, 1), jnp.int32),
               pltpu.VMEM((2, tq, 1), F32)]
    est = (3 * tq * 256 * 2 * 2 + 3 * s * 256 * 2 * 2 + tq * LANES * 4 * 2 + tq * s * 4
           + 12 * tq * 256 * 2 + 8 * tq * LANES * 4 + 6 * tq * 256 * 4 + 16 * tq * tq * 4)
    r3 = lambda a: a.reshape(b, s, -1)
    return pl.pallas_call(
        functools.partial(_dsa_kernel, tq=tq, top_k=top_k),
        grid=(b, nc),
        in_specs=[qspec(256), kspec(256), kspec(LANES), qspec(256), kspec(256), qspec(LANES)],
        out_specs=qspec(256),
        out_shape=jax.ShapeDtypeStruct((b, s, 256), BF16),
        scratch_shapes=scratch,
        compiler_params=pltpu.CompilerParams(dimension_semantics=("arbitrary",) * 2,
                                             vmem_limit_bytes=_vmem_limit(est)),
        name="dsa_attn",
    )(r3(dq), r3(dk), r3(dv), r3(qi), r3(ki), r3(wi))


def _dilated_kernel(q_ref, kp_ref, kc_ref, vp_ref, vc_ref, o_ref, lse_ref, *, w, nrb, nres):
    j = pl.program_id(2)
    lane = lax.broadcasted_iota(jnp.int32, (w, 256), 1)
    row = lax.broadcasted_iota(jnp.int32, (w, 2 * w), 0)
    col = lax.broadcasted_iota(jnp.int32, (w, 2 * w), 1)
    rel = row + w - col
    in_window = (rel >= 0) & (rel <= w)
    first_rows_ok = in_window & (col + j * (2 * w) >= w)
    heads = [(lane >> 6) == h for h in range(DIL_HEADS)]
    units = []
    for rs in range(nres):
        cs = slice(rs * 256, (rs + 1) * 256)
        for rb in range(nrb):
            rows = slice(rb * w, (rb + 1) * w)
            q_all = q_ref[0, rows, cs]
            if rb == 0:
                k_prev, v_prev, ok = kp_ref[0, :, cs], vp_ref[0, :, cs], first_rows_ok
            else:
                prev_rows = slice((rb - 1) * w, rb * w)
                k_prev, v_prev, ok = kc_ref[0, prev_rows, cs], vc_ref[0, prev_rows, cs], in_window
            k = jnp.concatenate([k_prev, kc_ref[0, rows, cs]], axis=0)
            v = jnp.concatenate([v_prev, vc_ref[0, rows, cs]], axis=0)
            scores = [jnp.where(ok, _dot_nt(jnp.where(head, q_all, jnp.zeros_like(q_all)), k), NEG_INF)
                      for head in heads]
            units.append((rows, cs, v, scores))
    soft = []
    for rows, cs, v, scores in units:
        stats = []
        for sc in scores:
            m = jnp.max(sc, axis=-1, keepdims=True)
            p = jnp.exp(sc - m)
            stats.append((m, jnp.sum(p, axis=-1, keepdims=True), p.astype(BF16)))
        soft.append((rows, cs, v, stats))
    for rows, cs, v, stats in soft:
        out = jnp.zeros((w, 256), F32)
        lse = jnp.zeros((w, 256), F32)
        for head, (m, l, p) in zip(heads, stats):
            out = jnp.where(head, _dot(p, v) / l, out)
            lse = jnp.where(head, m + jnp.log(l), lse)
        o_ref[0, rows, cs] = out
        lse_ref[0, rows, cs] = lse


def _dilated_attention(q, k, v, g, b, s):
    window, d = DIL_CONFIGS[g]
    w = window // d
    steps = s // d
    assert steps % w == 0
    nrb = min(4, steps // w)
    nres = min(d, max(1, 4 // nrb))
    q, k, v = [a.reshape(b, steps, d * 256) for a in (q, k, v)]
    cur = pl.BlockSpec((1, nrb * w, nres * 256), lambda bi, c, j: (bi, j, c))
    prev = pl.BlockSpec((1, w, nres * 256), lambda bi, c, j: (bi, jnp.maximum(j * nrb - 1, 0), c))
    est = (3 * nrb * w * nres * 256 * 2 * 2 + 2 * w * nres * 256 * 2 * 2 + 2 * nrb * w * nres * 256 * 4 * 2
           + 16 * 8 * w * 2 * w * 4)
    return pl.pallas_call(
        functools.partial(_dilated_kernel, w=w, nrb=nrb, nres=nres),
        grid=(b, d // nres, steps // (nrb * w)),
        in_specs=[cur, prev, cur, prev, cur],
        out_specs=[cur, cur],
        out_shape=[jax.ShapeDtypeStruct((b, steps, d * 256), F32)] * 2,
        compiler_params=pltpu.CompilerParams(dimension_semantics=("arbitrary",) * 3,
                                             vmem_limit_bytes=_vmem_limit(est)),
        name=f"dilated_attn_{g}",
    )(q, k, k, v, v)


def _layer_norm(y, g, b):
    mu = jnp.mean(y, axis=-1, keepdims=True)
    yc = y - mu
    var = jnp.mean(yc * yc, axis=-1, keepdims=True)
    return yc * lax.rsqrt(var + LN_EPS) * g + b


def _outproj_kernel(x_ref, oa_ref, ob_ref, o0_ref, l0_ref, o1_ref, l1_ref, o2_ref, l2_ref,
                    wo_ref, g_ref, b_ref, rwh_ref, rwl_ref, rb_ref, x1_ref, gate_ref, tok_scr):
    tm = x_ref.shape[0]

    def token_major(ref, slot, d):
        for j in range(2):
            for r in range(d):
                c0 = r * 256 + j * LANES
                tok_scr[2 * slot + j, pl.ds(r, tm // d, stride=d), :] = ref[0, :, c0:c0 + LANES]
        return jnp.concatenate([tok_scr[2 * slot], tok_scr[2 * slot + 1]], axis=1)

    d1, d2 = DIL_CONFIGS[1][1], DIL_CONFIGS[2][1]
    o0, l0 = o0_ref[...], l0_ref[...]
    o1, l1 = token_major(o1_ref, 0, d1), token_major(l1_ref, 1, d1)
    o2, l2 = token_major(o2_ref, 2, d2), token_major(l2_ref, 3, d2)
    mx = jnp.maximum(jnp.maximum(l0, l1), l2)
    e0, e1, e2 = jnp.exp(l0 - mx), jnp.exp(l1 - mx), jnp.exp(l2 - mx)
    oc = (e0 * o0 + e1 * o1 + e2 * o2) / (e0 + e1 + e2)
    na = MLA_HEADS * MLA_V
    mix = (_dot(oa_ref[...], wo_ref[0, :na, :]) + _dot(ob_ref[...], wo_ref[0, na:na + 256, :])
           + _dot(oc.astype(BF16), wo_ref[0, na + 256:, :]))
    x1 = _layer_norm(DEEPNORM_ALPHA * x_ref[...] + mix, g_ref[0], b_ref[0])
    x1_ref[...] = x1

    xh = x1.astype(BF16)
    xl = (x1 - xh.astype(F32)).astype(BF16)
    logits = _dot(xh, rwh_ref[0]) + _dot(xl, rwh_ref[0]) + _dot(xh, rwl_ref[0]) + rb_ref[0]
    lane = lax.broadcasted_iota(jnp.int32, (tm, LANES), 1)
    big = jnp.int32(LANES)
    is_group = (lane >= N_EXPERTS) & (lane < N_EXPERTS + N_GROUPS)
    gl = jnp.where(is_group, logits, -jnp.inf)
    gmax = jnp.max(gl, axis=-1, keepdims=True)
    g_star = jnp.min(jnp.where(gl == gmax, lane - N_EXPERTS, big), axis=-1, keepdims=True)
    p_top = 1.0 / jnp.sum(jnp.exp(gl - gmax), axis=-1, keepdims=True)
    in_group = (lane < N_EXPERTS) & ((lane >> 3) == g_star)
    el = jnp.where(in_group, logits, -jnp.inf)
    v1 = jnp.max(el, axis=-1, keepdims=True)
    i1 = jnp.min(jnp.where(el == v1, lane, big), axis=-1, keepdims=True)
    el2 = jnp.where(lane == i1, -jnp.inf, el)
    v2 = jnp.max(el2, axis=-1, keepdims=True)
    i2 = jnp.min(jnp.where(el2 == v2, lane, big), axis=-1, keepdims=True)
    e21 = jnp.exp(v2 - v1)
    w1 = 1.0 / (1.0 + e21)
    w2 = e21 / (1.0 + e21)
    gate = p_top * (jnp.where(lane == i1, w1, 0.0) + jnp.where(lane == i2, w2, 0.0))
    gate_ref[...] = jnp.where(lane == N_EXPERTS, g_star.astype(F32), gate)


def _outproj(layer, x2d, oa, ob, dil, w, tm, b, s):
    t = x2d.shape[0]
    row = lambda n: pl.BlockSpec((tm, n), lambda i: (i, 0))
    wspec = lambda a: pl.BlockSpec((1,) + a.shape[1:], lambda i: (layer,) + (0,) * (a.ndim - 1))
    tiles_per_seq = s // tm
    res_major = lambda d: pl.BlockSpec((1, tm // d, d * 256), lambda i: (i // tiles_per_seq, i % tiles_per_seq, 0))
    d1, d2 = DIL_CONFIGS[1][1], DIL_CONFIGS[2][1]
    weights = [w["wo"], w["ln1_g"], w["ln1_b"], w["rwh"], w["rwl"], w["rb"]]
    est = (D_MODEL * D_MODEL * 2 * 2 + 2 * D_MODEL * LANES * 2 * 2 + tm * D_MODEL * 4 * 4
           + tm * (512 + 256) * 2 * 2 + 6 * tm * 256 * 4 * 2 + 4 * tm * 256 * 4 + tm * LANES * 4 * 2
           + 6 * tm * D_MODEL * 4)
    return pl.pallas_call(
        _outproj_kernel,
        grid=(t // tm,),
        in_specs=[row(D_MODEL), row(MLA_HEADS * MLA_V), row(256), row(256), row(256),
                  res_major(d1), res_major(d1), res_major(d2), res_major(d2)] + [wspec(a) for a in weights],
        out_specs=[row(D_MODEL), row(LANES)],
        out_shape=[jax.ShapeDtypeStruct((t, D_MODEL), F32), jax.ShapeDtypeStruct((t, LANES), F32)],
        scratch_shapes=[pltpu.VMEM((8, tm, LANES), F32)],
        compiler_params=pltpu.CompilerParams(dimension_semantics=("arbitrary",),
                                             vmem_limit_bytes=_vmem_limit(est)),
        name="outproj_ln_router",
    )(x2d, oa, ob, dil[0][0].reshape(t, 256), dil[0][1].reshape(t, 256),
      dil[1][0], dil[1][1], dil[2][0], dil[2][1], *weights)


def _moe_kernel(x_ref, gate_ref, wg_ref, wu_ref, wd_ref, g_ref, b_ref, o_ref,
                xb_scr, acc_scr, tri_scr, rank_scr, rank_t_scr, load_scr, gate_hi_scr, gate_lo_scr, *, sub):
    g = pl.program_id(1)
    wt = x_ref.shape[0]
    lane = lax.broadcasted_iota(jnp.int32, (wt, LANES), 1)

    @pl.when((pl.program_id(0) == 0) & (g == 0))
    def _():
        before = lax.broadcasted_iota(jnp.int32, (wt, wt), 1) < lax.broadcasted_iota(jnp.int32, (wt, wt), 0)
        tri_scr[...] = jnp.where(before, 1.0, 0.0).astype(BF16)

    @pl.when(g == 0)
    def _():
        xb_scr[...] = x_ref[...].astype(BF16)
        acc_scr[...] = jnp.zeros_like(acc_scr)
        gate = gate_ref[...]
        gate_hi = gate.astype(BF16)
        gate_hi_scr[...] = gate_hi
        gate_lo_scr[...] = (gate - gate_hi.astype(F32)).astype(BF16)
        g_star = jnp.sum(jnp.where(lane == N_EXPERTS, gate, 0.0), axis=-1, keepdims=True)
        onehot = jnp.where(lane.astype(F32) == g_star, 1.0, 0.0)
        rank = jnp.where(onehot > 0.0, _dot(tri_scr[...], onehot.astype(BF16)), -1.0)
        rank_scr[...] = rank
        rank_t_scr[...] = jnp.transpose(rank)
        load_scr[...] = jnp.broadcast_to(jnp.sum(onehot, axis=0, keepdims=True), load_scr.shape)

    group_lane = lane == g
    rank_col = jnp.sum(jnp.where(group_lane, rank_scr[...], 0.0), axis=-1, keepdims=True)
    rank_row = rank_t_scr[pl.ds(g, 1), :]
    load_lane = lax.broadcasted_iota(jnp.int32, load_scr.shape, 1)
    load = jnp.sum(jnp.where(load_lane == g, load_scr[...], 0.0)) * (1.0 / load_scr.shape[0])
    n_sub = (load.astype(jnp.int32) + (sub - 1)) // sub
    lane_sub = lax.broadcasted_iota(jnp.int32, (sub, LANES), 1)
    slot_rows = lax.broadcasted_iota(jnp.int32, (sub, wt), 0).astype(F32)
    slot_cols = lax.broadcasted_iota(jnp.int32, (wt, sub), 1).astype(F32)

    def sub_tile(j, carry):
        base = (j * sub).astype(F32)
        take = jnp.where(rank_row == slot_rows + base, 1.0, 0.0).astype(BF16)
        xs = _dot(take, xb_scr[...]).astype(BF16)
        gs = _dot(take, gate_hi_scr[...]) + _dot(take, gate_lo_scr[...])
        y = jnp.zeros((sub, D_MODEL), F32)
        for e in range(EXPERTS_PER_GROUP):
            a = _dot(xs, wg_ref[0, 0, e])
            u = _dot(xs, wu_ref[0, 0, e])
            ge = jnp.sum(jnp.where(lane_sub == g * EXPERTS_PER_GROUP + e, gs, 0.0), axis=-1, keepdims=True)
            h = (a / (1.0 + jnp.exp(-a))) * u * ge
            y = y + _dot(h.astype(BF16), wd_ref[0, 0, e])
        put = jnp.where(rank_col == slot_cols + base, 1.0, 0.0).astype(BF16)
        acc_scr[...] += _dot(put, y.astype(BF16))
        return carry

    lax.fori_loop(0, n_sub, sub_tile, 0)

    @pl.when(g == pl.num_programs(1) - 1)
    def _():
        o_ref[...] = _layer_norm(DEEPNORM_ALPHA * x_ref[...] + acc_scr[...], g_ref[0], b_ref[0])


def _moe(layer, x1, gate, w, wt):
    t = x1.shape[0]
    f = EXPERT_HIDDEN
    epg = EXPERTS_PER_GROUP
    sub = min(wt, -(-(wt // N_GROUPS + wt // 32) // 16) * 16)
    est = (wt * D_MODEL * 4 * 4 + wt * LANES * 4 * 2 + 3 * epg * D_MODEL * f * 2 * 2
           + wt * D_MODEL * 6 + wt * wt * 2 + 6 * sub * D_MODEL * 4)
    wspec = lambda shape: pl.BlockSpec((1, 1) + shape, lambda i, g: (layer, g, 0, 0, 0))
    grouped = lambda a: a.reshape((a.shape[0], N_GROUPS, epg) + a.shape[2:])
    return pl.pallas_call(
        functools.partial(_moe_kernel, sub=sub),
        grid=(t // wt, N_GROUPS),
        in_specs=[pl.BlockSpec((wt, D_MODEL), lambda i, g: (i, 0)),
                  pl.BlockSpec((wt, LANES), lambda i, g: (i, 0)),
                  wspec((epg, D_MODEL, f)), wspec((epg, D_MODEL, f)), wspec((epg, f, D_MODEL)),
                  pl.BlockSpec((1, 1, D_MODEL), lambda i, g: (layer, 0, 0)),
                  pl.BlockSpec((1, 1, D_MODEL), lambda i, g: (layer, 0, 0))],
        out_specs=pl.BlockSpec((wt, D_MODEL), lambda i, g: (i, 0)),
        out_shape=jax.ShapeDtypeStruct((t, D_MODEL), F32),
        scratch_shapes=[pltpu.VMEM((wt, D_MODEL), BF16), pltpu.VMEM((wt, D_MODEL), F32),
                        pltpu.VMEM((wt, wt), BF16), pltpu.VMEM((wt, LANES), F32), pltpu.VMEM((LANES, wt), F32),
                        pltpu.VMEM((8, LANES), F32), pltpu.VMEM((wt, LANES), BF16), pltpu.VMEM((wt, LANES), BF16)],
        compiler_params=pltpu.CompilerParams(dimension_semantics=("arbitrary", "arbitrary"),
                                             vmem_limit_bytes=_vmem_limit(est)),
        name="moe_ln",
    )(x1, gate, grouped(w["wg"]), grouped(w["wu"]), grouped(w["wd"]), w["ln2_g"], w["ln2_b"])


def _pick_tile(n, pref):
    t = min(pref, n)
    while n % t:
        t //= 2
    return t


def _prepare_weights(w_in, mla_q_norm, mla_kv_norm, mla_w_uq, mla_w_ukv, w_o, ln1_g, ln1_b,
                     router_group_w, router_group_b, router_expert_w, router_expert_b,
                     expert_w_gate, expert_w_up, expert_w_down, ln2_g, ln2_b):
    depth = w_in.shape[0]
    w_perm = _gather_cols(w_in, _inproj_column_map()).astype(BF16)
    vec = lambda a: a.reshape(depth, 1, -1)
    rw = jnp.concatenate([jnp.moveaxis(router_expert_w, 1, 2).reshape(depth, D_MODEL, N_EXPERTS),
                          router_group_w,
                          jnp.zeros((depth, D_MODEL, LANES - N_EXPERTS - N_GROUPS), F32)], axis=-1)
    rb = jnp.concatenate([router_expert_b.reshape(depth, N_EXPERTS), router_group_b,
                          jnp.zeros((depth, LANES - N_EXPERTS - N_GROUPS), F32)], axis=-1)
    rwh = rw.astype(BF16)
    ex = lambda a: a.reshape((depth, N_EXPERTS) + a.shape[3:]).astype(BF16)
    return {
        "wa": w_perm[:, :, :_WA], "wb": w_perm[:, :, _WA:_WA + _WB], "wc": w_perm[:, :, _WA + _WB:],
        "wuq": _gather_cols(mla_w_uq, _uq_column_map()).astype(BF16),
        "wukv": _gather_cols(mla_w_ukv, _ukv_column_map()).astype(BF16),
        "qn": vec(mla_q_norm), "kvn": vec(mla_kv_norm),
        "wo": w_o.astype(BF16), "ln1_g": vec(ln1_g), "ln1_b": vec(ln1_b),
        "rwh": rwh, "rwl": (rw - rwh.astype(F32)).astype(BF16), "rb": vec(rb),
        "wg": ex(expert_w_gate), "wu": ex(expert_w_up), "wd": ex(expert_w_down),
        "ln2_g": vec(ln2_g), "ln2_b": vec(ln2_b),
    }


def kernel(x, positions, w_in, mla_q_norm, mla_kv_norm, mla_w_uq, mla_w_ukv, w_o, ln1_g, ln1_b,
           router_group_w, router_group_b, router_expert_w, router_expert_b,
           expert_w_gate, expert_w_up, expert_w_down, ln2_g, ln2_b):
    b, s, d_model = x.shape
    assert d_model == D_MODEL and w_in.shape[-1] == _IN_COLS
    t = b * s
    w = _prepare_weights(w_in, mla_q_norm, mla_kv_norm, mla_w_uq, mla_w_ukv, w_o, ln1_g, ln1_b,
                         router_group_w, router_group_b, router_expert_w, router_expert_b,
                         expert_w_gate, expert_w_up, expert_w_down, ln2_g, ln2_b)
    tabs = [a.reshape(t, LANES) for a in _rope_tables(positions)]
    tm_proj = _pick_tile(s, 512)
    tm_moe = _pick_tile(t, 1024)
    tq_mla = _pick_tile(s, 512)
    tq_dsa = _pick_tile(s, 256)
    n_cfg = len(DIL_CONFIGS)
    x2d = x.reshape(t, D_MODEL)
    for layer in range(w_in.shape[0]):
        (mq, mk, mv, dq, dk, dv, qi, ki, wi), dil_qkv = _inproj(layer, x2d, w, tabs, tm_proj, b, s)
        oa = _mla_attention(mq, mk, mv, b, s, tq_mla).reshape(t, -1)
        ob = _dsa_attention(dq, dk, dv, qi, ki, wi, b, s, tq_dsa).reshape(t, -1)
        dil = [_dilated_attention(dil_qkv[g], dil_qkv[n_cfg + g], dil_qkv[2 * n_cfg + g], g, b, s)
               for g in range(n_cfg)]
        x1, gate = _outproj(layer, x2d, oa, ob, dil, w, tm_proj, b, s)
        x2d = _moe(layer, x1, gate, w, tm_moe)
    return x2d.reshape(b, s, D_MODEL)
```
